```python
import jax, jax.numpy as jnp
from jax import lax
import numpy as np

D_MODEL = 2048
BATCH = 4
SEQ = 2048
DEPTH = 2

N_MEM = 256
MIX_WIDTH = D_MODEL
GM_WIDTH = MIX_WIDTH // 2
GM_HEAD_DIM = 128
GM_HEADS = GM_WIDTH // GM_HEAD_DIM
GM_CHUNK = 128
DN_WIDTH = MIX_WIDTH - GM_WIDTH
DN_HEAD_DIM = 128
DN_HEADS = DN_WIDTH // DN_HEAD_DIM
DN_CHUNK = 64
CONV_WIDTH = 4
XA_HEADS = 4
XA_HEAD_DIM = D_MODEL // XA_HEADS
D_FF = 5632
NORM_EPS = 1e-6
IN_COLS = 2 * GM_WIDTH + 4 * DN_WIDTH + 2 * DN_HEADS

kernel_name = "hybrid_sgu_deltanet_macaron_block"


def rmsnorm(x, g):
    xf = x.astype(jnp.float32)
    y = xf * lax.rsqrt(jnp.mean(xf * xf, axis=-1, keepdims=True) + NORM_EPS)
    return (y * g.astype(jnp.float32)).astype(x.dtype)


def l2norm(x):
    xf = x.astype(jnp.float32)
    return xf * lax.rsqrt(jnp.sum(xf * xf, axis=-1, keepdims=True) + NORM_EPS)


def swiglu_ffn(h, w_gate_up, w_down):
    gate, up = jnp.split(h @ w_gate_up, 2, axis=-1)
    return (jax.nn.silu(gate) * up) @ w_down


def causal_short_conv(x, w):
    k_width = w.shape[0]
    t_len = x.shape[1]
    xp = jnp.pad(x, ((0, 0), (k_width - 1, 0), (0, 0)))
    out = xp[:, 0:t_len] * w[0]
    for i in range(1, k_width):
        out = out + xp[:, i:i + t_len] * w[i]
    return out


def chunk_spatial_gating(u, v, sm_w, sm_b, ln_g, ln_b):
    b_, t_len, _ = u.shape
    n_chunks = t_len // GM_CHUNK
    u = jax.nn.gelu(u, approximate=False)
    vf = jax.nn.gelu(v, approximate=False).astype(jnp.float32)
    mu = jnp.mean(vf, axis=-1, keepdims=True)
    var = jnp.mean(jnp.square(vf - mu), axis=-1, keepdims=True)
    vn = ((vf - mu) * lax.rsqrt(var + NORM_EPS) * ln_g + ln_b).astype(u.dtype)
    vc = vn.reshape(b_, n_chunks, GM_CHUNK, GM_HEADS, GM_HEAD_DIM)
    causal = jnp.tril(jnp.ones((GM_CHUNK, GM_CHUNK), dtype=bool))
    w_masked = jnp.where(causal, sm_w, 0)
    mixed = jnp.einsum('hts,bnshd->bnthd', w_masked, vc) + sm_b.T[None, None, :, :, None]
    uc = u.reshape(b_, n_chunks, GM_CHUNK, GM_HEADS, GM_HEAD_DIM)
    return (uc * mixed).reshape(b_, t_len, GM_WIDTH)


def gated_delta_rule(q, k, v, g, beta):
    b_, t_len, h_, dk = q.shape
    dv = v.shape[-1]
    n_chunks = t_len // DN_CHUNK
    c = DN_CHUNK

    def to_chunks(a):
        return a.reshape(b_, n_chunks, c, h_, -1).transpose(1, 0, 3, 2, 4)

    qc = to_chunks(l2norm(q) * (dk ** -0.5))
    kc = to_chunks(l2norm(k))
    vc = to_chunks(v.astype(jnp.float32))
    gc = g.astype(jnp.float32).reshape(b_, n_chunks, c, h_).transpose(1, 0, 3, 2)
    bc = beta.astype(jnp.float32).reshape(b_, n_chunks, c, h_).transpose(1, 0, 3, 2)
    gcum = jnp.cumsum(gc, axis=-1)

    causal = jnp.tril(jnp.ones((c, c), dtype=bool))
    strict = jnp.tril(jnp.ones((c, c), dtype=bool), k=-1)
    decay = jnp.exp(jnp.where(causal, gcum[..., :, None] - gcum[..., None, :], -jnp.inf))

    k_beta = kc * bc[..., None]
    kkt = jnp.einsum('nbhtd,nbhsd->nbhts', k_beta, kc) * decay
    a_mat = jnp.eye(c, dtype=jnp.float32) + jnp.where(strict, kkt, 0.0)
    rhs = jnp.concatenate([vc * bc[..., None], k_beta * jnp.exp(gcum)[..., None]], axis=-1)
    sol = lax.linalg.triangular_solve(a_mat, rhs, left_side=True, lower=True, unit_diagonal=True)
    u_c, w_c = sol[..., :dv], sol[..., dv:]
    qk_intra = jnp.where(causal, jnp.einsum('nbhtd,nbhsd->nbhts', qc, kc) * decay, 0.0)

    def chunk_step(state, inp):
        q_i, k_i, u_i, w_i, g_i, a_i = inp
        v_new = u_i - jnp.einsum('bhck,bhkv->bhcv', w_i, state)
        o_i = (jnp.einsum('bhck,bhkv->bhcv', q_i * jnp.exp(g_i)[..., None], state)
               + jnp.einsum('bhts,bhsv->bhtv', a_i, v_new))
        g_last = g_i[..., -1]
        k_dec = k_i * jnp.exp(g_last[..., None] - g_i)[..., None]
        state = state * jnp.exp(g_last)[..., None, None] + jnp.einsum('bhck,bhcv->bhkv', k_dec, v_new)
        return state, o_i

    s0 = jnp.zeros((b_, h_, dk, dv), jnp.float32)
    _, o = lax.scan(chunk_step, s0, (qc, kc, u_c, w_c, gcum, qk_intra))
    return o.transpose(1, 0, 3, 2, 4).reshape(b_, t_len, h_, dv)


def token_mix(h, w_in, conv_w, a_log, dt_bias, sm_w, sm_b, sm_ln_g, sm_ln_b, dn_norm_w, w_out):
    b_, t_len, _ = h.shape
    proj = h @ w_in
    splits = [GM_WIDTH, 2 * GM_WIDTH, 2 * GM_WIDTH + 3 * DN_WIDTH,
              2 * GM_WIDTH + 4 * DN_WIDTH, 2 * GM_WIDTH + 4 * DN_WIDTH + DN_HEADS]
    u_a, v_a, qkv, z, b_raw, a_raw = jnp.split(proj, splits, axis=-1)

    y_a = chunk_spatial_gating(u_a, v_a, sm_w, sm_b, sm_ln_g, sm_ln_b)

    qkv = jax.nn.silu(causal_short_conv(qkv, conv_w))
    q, k, v = jnp.split(qkv, 3, axis=-1)
    q = q.reshape(b_, t_len, DN_HEADS, DN_HEAD_DIM)
    k = k.reshape(b_, t_len, DN_HEADS, DN_HEAD_DIM)
    v = v.reshape(b_, t_len, DN_HEADS, DN_HEAD_DIM)
    beta = jax.nn.sigmoid(b_raw.astype(jnp.float32))
    g = -jnp.exp(a_log.astype(jnp.float32)) * jax.nn.softplus(a_raw.astype(jnp.float32) + dt_bias.astype(jnp.float32))
    o = gated_delta_rule(q, k, v, g, beta)
    zf = z.astype(jnp.float32).reshape(b_, t_len, DN_HEADS, DN_HEAD_DIM)
    o = rmsnorm(o, dn_norm_w) * jax.nn.silu(zf)
    y_b = o.reshape(b_, t_len, DN_WIDTH).astype(h.dtype)

    return jnp.concatenate([y_a, y_b], axis=-1) @ w_out


def cross_attend(h, mem_h, w_xq, w_xkv, w_xo):
    b_, t_len, _ = h.shape
    m_len = mem_h.shape[1]
    q = (h @ w_xq).reshape(b_, t_len, XA_HEADS, XA_HEAD_DIM)
    kv = (mem_h @ w_xkv).reshape(b_, m_len, 2, XA_HEADS, XA_HEAD_DIM)
    k, v = kv[:, :, 0], kv[:, :, 1]
    s = jnp.einsum('bthd,bmhd->bhtm', q, k).astype(jnp.float32) * (XA_HEAD_DIM ** -0.5)
    p = jax.nn.softmax(s, axis=-1).astype(v.dtype)
    o = jnp.einsum('bhtm,bmhd->bthd', p, v).reshape(b_, t_len, XA_HEADS * XA_HEAD_DIM)
    return o @ w_xo


def setup_inputs(seed: int = 0) -> dict:
    key = jax.random.key(seed)
    ks = iter(jax.random.split(key, 64))
    L = DEPTH

    def dense(shape, fan_in):
        return jax.random.normal(next(ks), shape, jnp.float32) * (fan_in ** -0.5)

    def gain(shape):
        return 1.0 + 0.05 * jax.random.normal(next(ks), shape, jnp.float32)

    def small(shape):
        return 0.02 * jax.random.normal(next(ks), shape, jnp.float32)

    x = jax.random.normal(next(ks), (BATCH, SEQ, D_MODEL), jnp.float32)
    mem = jax.random.normal(next(ks), (BATCH, N_MEM, D_MODEL), jnp.float32)
    a_log = jnp.log(jax.random.uniform(next(ks), (L, DN_HEADS), jnp.float32, minval=1.0, maxval=16.0))
    dt = jnp.exp(jax.random.uniform(next(ks), (L, DN_HEADS), jnp.float32,
                                    minval=float(np.log(1e-3)), maxval=float(np.log(1e-1))))
    dt_bias = dt + jnp.log(-jnp.expm1(-dt))
    return {
        "x": x,
        "mem": mem,
        "ffn1_norm_pre": gain((L, D_MODEL)),
        "ffn1_w_gate_up": dense((L, D_MODEL, 2 * D_FF), D_MODEL),
        "ffn1_w_down": dense((L, D_FF, D_MODEL), D_FF),
        "ffn1_norm_post": gain((L, D_MODEL)),
        "mix_norm_pre": gain((L, D_MODEL)),
        "w_in": dense((L, D_MODEL, IN_COLS), D_MODEL),
        "conv_w": dense((L, CONV_WIDTH, 3 * DN_WIDTH), CONV_WIDTH),
        "a_log": a_log,
        "dt_bias": dt_bias,
        "sm_w": dense((L, GM_HEADS, GM_CHUNK, GM_CHUNK), GM_CHUNK),
        "sm_b": gain((L, GM_HEADS, GM_CHUNK)),
        "sm_ln_g": gain((L, GM_WIDTH)),
        "sm_ln_b": small((L, GM_WIDTH)),
        "dn_norm_w": gain((L, DN_HEAD_DIM)),
        "w_out": dense((L, MIX_WIDTH, D_MODEL), MIX_WIDTH),
        "mix_norm_post": gain((L, D_MODEL)),
        "xa_norm_pre": gain((L, D_MODEL)),
        "mem_norm": gain((L, D_MODEL)),
        "w_xq": dense((L, D_MODEL, D_MODEL), D_MODEL),
        "w_xkv": dense((L, D_MODEL, 2 * D_MODEL), D_MODEL),
        "w_xo": dense((L, D_MODEL, D_MODEL), D_MODEL),
        "xa_norm_post": gain((L, D_MODEL)),
        "ffn2_norm_pre": gain((L, D_MODEL)),
        "ffn2_w_gate_up": dense((L, D_MODEL, 2 * D_FF), D_MODEL),
        "ffn2_w_down": dense((L, D_FF, D_MODEL), D_FF),
        "ffn2_norm_post": gain((L, D_MODEL)),
    }


def reference(x, mem, ffn1_norm_pre, ffn1_w_gate_up, ffn1_w_down, ffn1_norm_post,
              mix_norm_pre, w_in, conv_w, a_log, dt_bias, sm_w, sm_b, sm_ln_g, sm_ln_b,
              dn_norm_w, w_out, mix_norm_post, xa_norm_pre, mem_norm, w_xq, w_xkv, w_xo,
              xa_norm_post, ffn2_norm_pre, ffn2_w_gate_up, ffn2_w_down, ffn2_norm_post):
    for l in range(DEPTH):
        f = swiglu_ffn(rmsnorm(x, ffn1_norm_pre[l]), ffn1_w_gate_up[l], ffn1_w_down[l])
        x = x + 0.5 * rmsnorm(f, ffn1_norm_post[l])
        m = token_mix(rmsnorm(x, mix_norm_pre[l]), w_in[l], conv_w[l], a_log[l], dt_bias[l],
                      sm_w[l], sm_b[l], sm_ln_g[l], sm_ln_b[l], dn_norm_w[l], w_out[l])
        x = x + rmsnorm(m, mix_norm_post[l])
        c = cross_attend(rmsnorm(x, xa_norm_pre[l]), rmsnorm(mem, mem_norm[l]), w_xq[l], w_xkv[l], w_xo[l])
        x = x + rmsnorm(c, xa_norm_post[l])
        f = swiglu_ffn(rmsnorm(x, ffn2_norm_pre[l]), ffn2_w_gate_up[l], ffn2_w_down[l])
        x = x + 0.5 * rmsnorm(f, ffn2_norm_post[l])
    return x
```

```python
import functools

import jax
import jax.numpy as jnp
from jax import lax
from jax.experimental import pallas as pl
from jax.experimental.pallas import tpu as pltpu

F32 = jnp.float32
BF16 = jnp.bfloat16

NORM_EPS = 1e-6
HEAD_DIM = 128
GM_HEADS = 8
DN_HEADS = 8
GM_CHUNK = 128
DN_CHUNK = 64
CONV_WIDTH = 4
XA_HEADS = 4
LANES = 128
VMEM_LIMIT_BYTES = 56 * 1024 * 1024

_U_GRP, _V_GRP, _Q_GRP, _K_GRP, _VV_GRP, _Z_GRP = 0, 8, 16, 24, 32, 40
_N_GRP = 48


def _params(*sem):
    return pltpu.CompilerParams(dimension_semantics=sem, vmem_limit_bytes=VMEM_LIMIT_BYTES)


def _tile(dim, want):
    t = min(dim, want)
    assert dim % t == 0, (dim, want)
    return t


def _rms(x, g):
    return x * lax.rsqrt(jnp.mean(x * x, axis=-1, keepdims=True) + NORM_EPS) * g


def _silu(x):
    return x * jax.nn.sigmoid(x)


def _gelu(x):
    return 0.5 * x * (1.0 + lax.erf(x * (2.0 ** -0.5)))


def _rmsnorm_kernel(x_ref, g_ref, o_ref):
    o_ref[...] = _rms(x_ref[...], g_ref[...]).astype(o_ref.dtype)


def rmsnorm(x, g, out_dtype=BF16, tm=512):
    m, d = x.shape
    tm = _tile(m, tm)
    return pl.pallas_call(
        _rmsnorm_kernel,
        grid=(m // tm,),
        in_specs=[pl.BlockSpec((tm, d), lambda i: (i, 0)),
                  pl.BlockSpec((1, d), lambda i: (0, 0))],
        out_specs=pl.BlockSpec((tm, d), lambda i: (i, 0)),
        out_shape=jax.ShapeDtypeStruct((m, d), out_dtype),
        compiler_params=_params("parallel"),
        name="rmsnorm",
    )(x, g.reshape(1, d))


def _mm_ws_kernel(*refs, n_w, mode):
    x_ref = refs[0]
    w_refs = refs[1:1 + n_w]
    o_ref = refs[1 + n_w]
    wb_refs = refs[2 + n_w:]

    @pl.when(pl.program_id(1) == 0)
    def _():
        for w_ref, wb_ref in zip(w_refs, wb_refs):
            wb_ref[...] = w_ref[...].astype(BF16)

    x = x_ref[...]
    ys = [jnp.dot(x, wb_ref[...], preferred_element_type=F32) for wb_ref in wb_refs]
    if mode == "swiglu":
        gate, up = ys
        o_ref[...] = (_silu(gate) * up).astype(o_ref.dtype)
    elif mode == "headmajor":
        y = ys[0]
        for c in range(y.shape[1] // LANES):
            o_ref[c] = y[:, c * LANES:(c + 1) * LANES].astype(o_ref.dtype)
    else:
        o_ref[...] = ys[0].astype(o_ref.dtype)


def matmul_ws(x, w, *, mode="plain", out_dtype=BF16, tm=1024, tn=512, n_cols=None, col_offsets=(0,)):
    m, k = x.shape
    n = w.shape[1] if n_cols is None else n_cols
    tm, tn = _tile(m, tm), _tile(n, tn)
    n_w = len(col_offsets)
    assert all(off % tn == 0 for off in col_offsets)

    def w_spec(off):
        return pl.BlockSpec((k, tn), lambda j, i: (0, j + off // tn))

    if mode == "headmajor":
        out_shape = jax.ShapeDtypeStruct((n // LANES, m, LANES), out_dtype)
        out_spec = pl.BlockSpec((tn // LANES, tm, LANES), lambda j, i: (j, i, 0))
    else:
        out_shape = jax.ShapeDtypeStruct((m, n), out_dtype)
        out_spec = pl.BlockSpec((tm, tn), lambda j, i: (i, j))
    return pl.pallas_call(
        functools.partial(_mm_ws_kernel, n_w=n_w, mode=mode),
        grid=(n // tn, m // tm),
        in_specs=[pl.BlockSpec((tm, k), lambda j, i: (i, 0))] + [w_spec(off) for off in col_offsets],
        out_specs=out_spec,
        out_shape=out_shape,
        scratch_shapes=[pltpu.VMEM((k, tn), BF16) for _ in range(n_w)],
        compiler_params=_params("arbitrary", "arbitrary"),
        name="mm_ws_" + mode,
    )(x, *([w] * n_w))


def _post_epilogue(f, x, gp, gn, scale, x_out_ref, xn_out_ref):
    x_new = x + scale * _rms(f, gp)
    x_out_ref[...] = x_new
    if xn_out_ref is not None:
        xn_out_ref[...] = _rms(x_new, gn).astype(xn_out_ref.dtype)


def _mm_post_kernel(*refs, n_act, n_k, scale, has_next):
    act_refs = refs[:n_act]
    w_refs = refs[n_act:2 * n_act]
    x_ref, gp_ref = refs[2 * n_act:2 * n_act + 2]
    rest = refs[2 * n_act + 2:]
    if has_next:
        gn_ref, x_out_ref, xn_out_ref = rest[:3]
        rest = rest[3:]
    else:
        gn_ref, xn_out_ref = None, None
        x_out_ref = rest[0]
        rest = rest[1:]

    part = None
    for a_ref, w_ref in zip(act_refs, w_refs):
        d = jnp.dot(a_ref[...], w_ref[...], preferred_element_type=F32)
        part = d if part is None else part + d

    def finish(f):
        _post_epilogue(f, x_ref[...], gp_ref[...], None if gn_ref is None else gn_ref[...],
                       scale, x_out_ref, xn_out_ref)

    if n_k == 1:
        finish(part)
    else:
        acc_ref = rest[0]
        kk = pl.program_id(1)

        @pl.when(kk == 0)
        def _():
            acc_ref[...] = part

        @pl.when(jnp.logical_and(kk > 0, kk < n_k - 1))
        def _():
            acc_ref[...] += part

        @pl.when(kk == n_k - 1)
        def _():
            finish(acc_ref[...] + part)


def matmul_post(acts, ws, x, g_post, g_next, *, scale, tm=512, tk=None):
    m, d = x.shape
    tm = _tile(m, tm)
    n_act = len(acts)
    has_next = g_next is not None
    if tk is None:
        n_k = 1
        act_specs = [pl.BlockSpec((tm, a.shape[1]), lambda i, kk: (i, 0)) for a in acts]
        w_specs = [pl.BlockSpec(w.shape, lambda i, kk: (0, 0)) for w in ws]
        scratch = []
    else:
        assert n_act == 1
        k = acts[0].shape[1]
        tk = _tile(k, tk)
        n_k = k // tk
        act_specs = [pl.BlockSpec((tm, tk), lambda i, kk: (i, kk))]
        w_specs = [pl.BlockSpec((tk, d), lambda i, kk: (kk, 0))]
        scratch = [pltpu.VMEM((tm, d), F32)]
    row = pl.BlockSpec((tm, d), lambda i, kk: (i, 0))
    vec = pl.BlockSpec((1, d), lambda i, kk: (0, 0))
    in_specs = act_specs + w_specs + [row, vec] + ([vec] if has_next else [])
    args = list(acts) + list(ws) + [x, g_post.reshape(1, d)] + ([g_next.reshape(1, d)] if has_next else [])
    out_shape = [jax.ShapeDtypeStruct((m, d), F32)]
    out_specs = [row]
    if has_next:
        out_shape.append(jax.ShapeDtypeStruct((m, d), BF16))
        out_specs.append(row)
    outs = pl.pallas_call(
        functools.partial(_mm_post_kernel, n_act=n_act, n_k=n_k, scale=scale, has_next=has_next),
        grid=(m // tm, n_k),
        in_specs=in_specs,
        out_specs=out_specs,
        out_shape=out_shape,
        scratch_shapes=scratch,
        compiler_params=_params("parallel", "arbitrary"),
        name="mm_post",
    )(*args)
    return (outs[0], outs[1]) if has_next else (outs[0], None)


def _sgu_kernel(u_ref, v_ref, w_ref, bt_ref, lg_ref, lb_ref, o_ref, *, n_chunks):
    c = GM_CHUNK
    width = GM_HEADS * HEAD_DIM
    row = lax.broadcasted_iota(jnp.int32, (c, c), 0)
    col = lax.broadcasted_iota(jnp.int32, (c, c), 1)
    causal = row >= col
    vg = [_gelu(v_ref[h]) for h in range(GM_HEADS)]
    mu = sum(jnp.sum(x, axis=-1, keepdims=True) for x in vg) * (1.0 / width)
    var = sum(jnp.sum(jnp.square(x - mu), axis=-1, keepdims=True) for x in vg) * (1.0 / width)
    rstd = lax.rsqrt(var + NORM_EPS)
    for h in range(GM_HEADS):
        vn = ((vg[h] - mu) * rstd * lg_ref[h] + lb_ref[h]).astype(BF16)
        w = jnp.where(causal, w_ref[h], 0.0).astype(BF16)
        rhs = jnp.concatenate([vn[n * c:(n + 1) * c] for n in range(n_chunks)], axis=1)
        mixed = jnp.dot(w, rhs, preferred_element_type=F32) + bt_ref[:, h:h + 1]
        ug = _gelu(u_ref[h])
        for n in range(n_chunks):
            y = ug[n * c:(n + 1) * c] * mixed[:, n * HEAD_DIM:(n + 1) * HEAD_DIM]
            o_ref[n * c:(n + 1) * c, h * HEAD_DIM:(h + 1) * HEAD_DIM] = y.astype(o_ref.dtype)


def sgu(proj_hm, sm_w, sm_b, ln_g, ln_b, *, tb=256):
    _, m, _ = proj_hm.shape
    tb = _tile(m, tb)
    width = GM_HEADS * HEAD_DIM
    return pl.pallas_call(
        functools.partial(_sgu_kernel, n_chunks=tb // GM_CHUNK),
        grid=(m // tb,),
        in_specs=[pl.BlockSpec((GM_HEADS, tb, HEAD_DIM), lambda i: (_U_GRP // GM_HEADS, i, 0)),
                  pl.BlockSpec((GM_HEADS, tb, HEAD_DIM), lambda i: (_V_GRP // GM_HEADS, i, 0)),
                  pl.BlockSpec((GM_HEADS, GM_CHUNK, GM_CHUNK), lambda i: (0, 0, 0)),
                  pl.BlockSpec((GM_CHUNK, GM_HEADS), lambda i: (0, 0)),
                  pl.BlockSpec((GM_HEADS, 1, HEAD_DIM), lambda i: (0, 0, 0)),
                  pl.BlockSpec((GM_HEADS, 1, HEAD_DIM), lambda i: (0, 0, 0))],
        out_specs=pl.BlockSpec((tb, width), lambda i: (i, 0)),
        out_shape=jax.ShapeDtypeStruct((m, width), BF16),
        compiler_params=_params("parallel"),
        name="sgu",
    )(proj_hm, proj_hm, sm_w, sm_b.T, ln_g.reshape(GM_HEADS, 1, HEAD_DIM), ln_b.reshape(GM_HEADS, 1, HEAD_DIM))


def _gates_kernel(ba_ref, alog_ref, dtb_ref, g_ref, gt_ref, *, n_chunks):
    c = DN_CHUNK
    ba = ba_ref[...]
    lane = lax.broadcasted_iota(jnp.int32, ba.shape, 1)
    beta = jax.nn.sigmoid(ba)
    g = -jnp.exp(alog_ref[...]) * jax.nn.softplus(ba + dtb_ref[...])
    row = lax.broadcasted_iota(jnp.int32, (c, c), 0)
    col = lax.broadcasted_iota(jnp.int32, (c, c), 1)
    tri = (row >= col).astype(F32)
    gcum = jnp.concatenate(
        [jnp.dot(tri, g[n * c:(n + 1) * c], preferred_element_type=F32, precision=lax.Precision.HIGHEST)
         for n in range(n_chunks)], axis=0)
    out = jnp.where(lane < DN_HEADS, beta, gcum)
    g_ref[...] = out
    gt_ref[...] = out.T


def gates(ba, a_log, dt_bias, *, tb=256):
    m = ba.shape[0]
    tb = _tile(m, tb)
    pad = jnp.zeros((1, LANES), F32)
    alog_row = lax.dynamic_update_slice(pad, a_log.reshape(1, DN_HEADS), (0, DN_HEADS))
    dtb_row = lax.dynamic_update_slice(pad, dt_bias.reshape(1, DN_HEADS), (0, DN_HEADS))
    return pl.pallas_call(
        functools.partial(_gates_kernel, n_chunks=tb // DN_CHUNK),
        grid=(m // tb,),
        in_specs=[pl.BlockSpec((tb, LANES), lambda i: (i, 0)),
                  pl.BlockSpec((1, LANES), lambda i: (0, 0)),
                  pl.BlockSpec((1, LANES), lambda i: (0, 0))],
        out_specs=[pl.BlockSpec((tb, LANES), lambda i: (i, 0)),
                   pl.BlockSpec((LANES, tb), lambda i: (0, i))],
        out_shape=[jax.ShapeDtypeStruct((m, LANES), F32),
                   jax.ShapeDtypeStruct((LANES, m), F32)],
        compiler_params=_params("parallel"),
        name="dn_gates",
    )(ba, alog_row, dtb_row)


def _dot_nt(a, b, **kw):
    return lax.dot_general(a, b, (((1,), (1,)), ((), ())), preferred_element_type=F32, **kw)


def _dot_tn(a, b, **kw):
    return lax.dot_general(a, b, (((0,), (0,)), ((), ())), preferred_element_type=F32, **kw)


def _unit_lower_inverse(strict_lower):
    c = strict_lower.shape[0]
    hi = lax.Precision.HIGHEST
    row = lax.broadcasted_iota(jnp.int32, (c, c), 0)
    col = lax.broadcasted_iota(jnp.int32, (c, c), 1)
    eye = (row == col).astype(F32)
    n_pow = -strict_lower
    inv = eye + n_pow
    span = 2
    while span < c:
        n_pow = jnp.dot(n_pow, n_pow, preferred_element_type=F32, precision=hi)
        inv = inv + jnp.dot(inv, n_pow, preferred_element_type=F32, precision=hi)
        span *= 2
    return inv


def _deltanet_kernel(q_ref, k_ref, v_ref, z_ref, g_ref, gt_ref, cwq_ref, cwk_ref, cwv_ref, nw_ref,
                     o_ref, state_ref, xbuf_ref, *, n_chunks):
    c = DN_CHUNK
    tb = n_chunks * c
    hd = HEAD_DIM
    h = pl.program_id(1)

    @pl.when(pl.program_id(2) == 0)
    def _():
        state_ref[...] = jnp.zeros_like(state_ref)
        xbuf_ref[:, 0:8, :] = jnp.zeros((3, 8, hd), F32)

    def conv_silu(p, x_ref, cw_ref):
        xbuf_ref[p, 8:8 + tb, :] = x_ref[0]
        y = cw_ref[0, 0:1, :] * xbuf_ref[p, 5:5 + tb, :]
        for i in range(1, CONV_WIDTH):
            y = y + cw_ref[0, i:i + 1, :] * xbuf_ref[p, 5 + i:5 + i + tb, :]
        xbuf_ref[p, 0:8, :] = xbuf_ref[p, tb:tb + 8, :]
        return _silu(y)

    q = conv_silu(0, q_ref, cwq_ref)
    k = conv_silu(1, k_ref, cwk_ref)
    v = conv_silu(2, v_ref, cwv_ref)
    qn = q * lax.rsqrt(jnp.sum(q * q, axis=-1, keepdims=True) + NORM_EPS) * (hd ** -0.5)
    kn = k * lax.rsqrt(jnp.sum(k * k, axis=-1, keepdims=True) + NORM_EPS)

    gates_blk = g_ref[...]
    lane = lax.broadcasted_iota(jnp.int32, gates_blk.shape, 1)
    beta = jnp.sum(jnp.where(lane == h, gates_blk, 0.0), axis=-1, keepdims=True)
    gc = jnp.sum(jnp.where(lane == h + DN_HEADS, gates_blk, 0.0), axis=-1, keepdims=True)
    gc_row = gt_ref[pl.ds(h + DN_HEADS, 1), :]

    row = lax.broadcasted_iota(jnp.int32, (c, c), 0)
    col = lax.broadcasted_iota(jnp.int32, (c, c), 1)
    causal = row >= col
    strict = row > col

    state = state_ref[...]
    for n in range(n_chunks):
        sl = slice(n * c, (n + 1) * c)
        qn_c, kn_c, v_c = qn[sl], kn[sl], v[sl]
        beta_c, gc_c = beta[sl], gc[sl]
        gr_c = gc_row[:, n * c:(n + 1) * c]
        decay = jnp.where(causal, jnp.exp(jnp.where(causal, gc_c - gr_c, 0.0)), 0.0)
        eg = jnp.exp(gc_c)
        kb = kn_c * beta_c
        kn_b = kn_c.astype(BF16)
        kkt = _dot_nt(kb.astype(BF16), kn_b) * decay
        inv = _unit_lower_inverse(jnp.where(strict, kkt, 0.0))
        rhs = jnp.concatenate([v_c * beta_c, kb * eg], axis=1)
        sol = jnp.dot(inv, rhs, preferred_element_type=F32, precision=lax.Precision.HIGHEST)
        u_c, w_c = sol[:, :hd], sol[:, hd:]
        qk = jnp.where(causal, _dot_nt(qn_c.astype(BF16), kn_b) * decay, 0.0)

        state_b = state.astype(BF16)
        v_new = u_c - jnp.dot(w_c.astype(BF16), state_b, preferred_element_type=F32)
        o_c = (jnp.dot((qn_c * eg).astype(BF16), state_b, preferred_element_type=F32)
               + jnp.dot(qk.astype(BF16), v_new.astype(BF16), preferred_element_type=F32))
        g_last = gc_c[c - 1:c, :]
        k_dec = kn_c * jnp.exp(g_last - gc_c)
        state = state * jnp.exp(g_last) + _dot_tn(k_dec.astype(BF16), v_new.astype(BF16))

        zf = z_ref[0, sl, :]
        y = _rms(o_c, nw_ref[...]) * _silu(zf)
        o_ref[sl, :] = y.astype(o_ref.dtype)
    state_ref[...] = state


def deltanet(proj_hm, gates_arr, gates_t, conv_w, dn_norm_w, *, batch, tb=256):
    _, m, _ = proj_hm.shape
    t_len = m // batch
    tb = _tile(t_len, tb)
    n_t = t_len // tb
    hd = HEAD_DIM
    cw = conv_w.reshape(CONV_WIDTH, 3 * DN_HEADS, hd).transpose(1, 0, 2)

    def grp(base):
        return pl.BlockSpec((1, tb, hd), lambda b, h, t: (base + h, b * n_t + t, 0))

    def cw_spec(base):
        return pl.BlockSpec((1, CONV_WIDTH, hd), lambda b, h, t: (base + h, 0, 0))

    return pl.pallas_call(
        functools.partial(_deltanet_kernel, n_chunks=tb // DN_CHUNK),
        grid=(batch, DN_HEADS, n_t),
        in_specs=[grp(_Q_GRP), grp(_K_GRP), grp(_VV_GRP), grp(_Z_GRP),
                  pl.BlockSpec((tb, LANES), lambda b, h, t: (b * n_t + t, 0)),
                  pl.BlockSpec((2 * DN_HEADS, tb), lambda b, h, t: (0, b * n_t + t)),
                  cw_spec(0), cw_spec(DN_HEADS), cw_spec(2 * DN_HEADS),
                  pl.BlockSpec((1, hd), lambda b, h, t: (0, 0))],
        out_specs=pl.BlockSpec((tb, hd), lambda b, h, t: (b * n_t + t, h)),
        out_shape=jax.ShapeDtypeStruct((m, DN_HEADS * hd), BF16),
        scratch_shapes=[pltpu.VMEM((hd, hd), F32),
                        pltpu.VMEM((3, tb + 8, hd), F32)],
        compiler_params=_params("parallel", "parallel", "arbitrary"),
        name="deltanet",
    )(proj_hm, proj_hm, proj_hm, proj_hm, gates_arr, gates_t, cw, cw, cw, dn_norm_w.reshape(1, hd))


def _xattn_kernel(q_ref, k_ref, v_ref, o_ref, *, head_dim):
    scale = head_dim ** -0.5
    for h in range(XA_HEADS):
        sl = slice(h * head_dim, (h + 1) * head_dim)
        s = _dot_nt(q_ref[:, sl], k_ref[:, sl]) * scale
        s = s - jnp.max(s, axis=-1, keepdims=True)
        e = jnp.exp(s)
        p = e / jnp.sum(e, axis=-1, keepdims=True)
        o = jnp.dot(p.astype(BF16), v_ref[:, sl], preferred_element_type=F32)
        o_ref[:, sl] = o.astype(o_ref.dtype)


def xattn(q, kv, *, batch, tm=512):
    m, d = q.shape
    t_len = m // batch
    n_mem = kv.shape[0] // batch
    tm = _tile(t_len, tm)
    n_t = t_len // tm
    return pl.pallas_call(
        functools.partial(_xattn_kernel, head_dim=d // XA_HEADS),
        grid=(batch, n_t),
        in_specs=[pl.BlockSpec((tm, d), lambda b, t: (b * n_t + t, 0)),
                  pl.BlockSpec((n_mem, d), lambda b, t: (b, 0)),
                  pl.BlockSpec((n_mem, d), lambda b, t: (b, 1))],
        out_specs=pl.BlockSpec((tm, d), lambda b, t: (b * n_t + t, 0)),
        out_shape=jax.ShapeDtypeStruct((m, d), BF16),
        compiler_params=_params("parallel", "parallel"),
        name="xattn",
    )(q, kv, kv)


def kernel(x, mem, ffn1_norm_pre, ffn1_w_gate_up, ffn1_w_down, ffn1_norm_post, mix_norm_pre, w_in, conv_w, a_log, dt_bias, sm_w, sm_b, sm_ln_g, sm_ln_b, dn_norm_w, w_out, mix_norm_post, xa_norm_pre, mem_norm, w_xq, w_xkv, w_xo, xa_norm_post, ffn2_norm_pre, ffn2_w_gate_up, ffn2_w_down, ffn2_norm_post):
    batch, t_len, d = x.shape
    depth = w_in.shape[0]
    d_ff = ffn1_w_down.shape[1]
    n_main = _N_GRP * LANES
    gm_width = GM_HEADS * HEAD_DIM
    x2 = x.reshape(batch * t_len, d)
    mem2 = mem.reshape(-1, d)

    def ffn(xn, x2, w_gate_up, w_down, g_post, g_next):
        hmid = matmul_ws(xn, w_gate_up, mode="swiglu", n_cols=d_ff, col_offsets=(0, d_ff))
        return matmul_post([hmid], [w_down.astype(BF16)], x2, g_post, g_next, scale=0.5, tk=512)

    xn = rmsnorm(x2, ffn1_norm_pre[0])
    for l in range(depth):
        x2, xn = ffn(xn, x2, ffn1_w_gate_up[l], ffn1_w_down[l], ffn1_norm_post[l], mix_norm_pre[l])

        proj_hm = matmul_ws(xn, w_in[l], mode="headmajor", out_dtype=F32, n_cols=n_main)
        w_small = jnp.pad(w_in[l][:, n_main:], ((0, 0), (0, LANES - 2 * DN_HEADS)))
        ba = matmul_ws(xn, w_small, out_dtype=F32, tn=LANES)
        y_a = sgu(proj_hm, sm_w[l], sm_b[l], sm_ln_g[l], sm_ln_b[l])
        gates_arr, gates_t = gates(ba, a_log[l], dt_bias[l])
        y_b = deltanet(proj_hm, gates_arr, gates_t, conv_w[l], dn_norm_w[l], batch=batch)
        w_o = w_out[l].astype(BF16)
        x2, xn = matmul_post([y_a, y_b], [w_o[:gm_width], w_o[gm_width:]], x2,
                             mix_norm_post[l], xa_norm_pre[l], scale=1.0)

        q = matmul_ws(xn, w_xq[l])
        kv = matmul_ws(rmsnorm(mem2, mem_norm[l]), w_xkv[l])
        att = xattn(q, kv, batch=batch)
        x2, xn = matmul_post([att], [w_xo[l].astype(BF16)], x2, xa_norm_post[l], ffn2_norm_pre[l], scale=1.0)

        g_next = ffn1_norm_pre[l + 1] if l + 1 < depth else None
        x2, xn = ffn(xn, x2, ffn2_w_gate_up[l], ffn2_w_down[l], ffn2_norm_post[l], g_next)
    return x2.reshape(batch, t_len, d)
```

```python
import functools

import jax
import jax.numpy as jnp
from jax import lax
from jax.experimental import pallas as pl
from jax.experimental.pallas import tpu as pltpu

F32 = jnp.float32
BF16 = jnp.bfloat16

NORM_EPS = 1e-6
HEAD_DIM = 128
GM_HEADS = 8
DN_HEADS = 8
GM_CHUNK = 128
DN_CHUNK = 64
CONV_WIDTH = 4
XA_HEADS = 4
LANES = 128
VMEM_LIMIT_BYTES = 58 * 1024 * 1024

_U_GRP, _V_GRP, _Q_GRP, _K_GRP, _VV_GRP, _Z_GRP = 0, 8, 16, 24, 32, 40
_N_GRP = 48
_BETA, _GCUM, _EG, _EDEC = 0, 1, 2, 3
_N_DECAY_COPIES = 3


def _params(*sem):
    return pltpu.CompilerParams(dimension_semantics=sem, vmem_limit_bytes=VMEM_LIMIT_BYTES)


def _tile(dim, want):
    t = min(dim, want)
    assert dim % t == 0, (dim, want)
    return t


def _rms(x, g):
    return x * lax.rsqrt(jnp.mean(x * x, axis=-1, keepdims=True) + NORM_EPS) * g


def _silu(x):
    return x * jax.nn.sigmoid(x)


def _gelu(x):
    return 0.5 * x * (1.0 + lax.erf(x * (2.0 ** -0.5)))


def _rmsnorm_kernel(x_ref, g_ref, o_ref):
    o_ref[...] = _rms(x_ref[...], g_ref[...]).astype(o_ref.dtype)


def rmsnorm(x, g_stack, layer, out_dtype=BF16, tm=512):
    m, d = x.shape
    tm = _tile(m, tm)
    return pl.pallas_call(
        _rmsnorm_kernel,
        grid=(m // tm,),
        in_specs=[pl.BlockSpec((tm, d), lambda i: (i, 0)),
                  pl.BlockSpec((None, 1, d), lambda i: (layer, 0, 0))],
        out_specs=pl.BlockSpec((tm, d), lambda i: (i, 0)),
        out_shape=jax.ShapeDtypeStruct((m, d), out_dtype),
        compiler_params=_params("parallel"),
        name="rmsnorm",
    )(x, g_stack.reshape(g_stack.shape[0], 1, d))


def _mm_ws_kernel(*refs, n_w, mode):
    x_ref = refs[0]
    w_refs = refs[1:1 + n_w]
    o_ref = refs[1 + n_w]
    wb_refs = refs[2 + n_w:]

    @pl.when(pl.program_id(1) == 0)
    def _():
        for w_ref, wb_ref in zip(w_refs, wb_refs):
            wb_ref[...] = w_ref[...].astype(BF16)

    x = x_ref[...]
    ys = [jnp.dot(x, wb_ref[...], preferred_element_type=F32) for wb_ref in wb_refs]
    if mode == "swiglu":
        gate, up = ys
        o_ref[...] = (_silu(gate) * up).astype(o_ref.dtype)
    elif mode == "headmajor":
        y = ys[0]
        for c in range(y.shape[1] // LANES):
            o_ref[c] = y[:, c * LANES:(c + 1) * LANES].astype(o_ref.dtype)
    else:
        o_ref[...] = ys[0].astype(o_ref.dtype)


def matmul_ws(x, w_stack, layer, *, mode="plain", out_dtype=BF16, tm=1024, tn=512, n_cols=None, col_offsets=(0,)):
    m, k = x.shape
    n = w_stack.shape[2] if n_cols is None else n_cols
    tm, tn = _tile(m, tm), _tile(n, tn)
    n_w = len(col_offsets)
    assert all(off % tn == 0 for off in col_offsets)

    def w_spec(off):
        return pl.BlockSpec((None, k, tn), lambda j, i: (layer, 0, j + off // tn))

    if mode == "headmajor":
        out_shape = jax.ShapeDtypeStruct((n // LANES, m, LANES), out_dtype)
        out_spec = pl.BlockSpec((tn // LANES, tm, LANES), lambda j, i: (j, i, 0))
    else:
        out_shape = jax.ShapeDtypeStruct((m, n), out_dtype)
        out_spec = pl.BlockSpec((tm, tn), lambda j, i: (i, j))
    return pl.pallas_call(
        functools.partial(_mm_ws_kernel, n_w=n_w, mode=mode),
        grid=(n // tn, m // tm),
        in_specs=[pl.BlockSpec((tm, k), lambda j, i: (i, 0))] + [w_spec(off) for off in col_offsets],
        out_specs=out_spec,
        out_shape=out_shape,
        scratch_shapes=[pltpu.VMEM((k, tn), BF16) for _ in range(n_w)],
        compiler_params=_params("arbitrary", "arbitrary"),
        name="mm_ws_" + mode,
    )(x, *([w_stack] * n_w))


def _mm_post_kernel(*refs, n_act, n_sub, scale, has_next):
    act_refs = refs[:n_act]
    w_refs = refs[n_act:2 * n_act]
    x_ref, gp_ref = refs[2 * n_act:2 * n_act + 2]
    rest = refs[2 * n_act + 2:]
    if has_next:
        gn_ref, x_out_ref, xn_out_ref = rest
    else:
        gn_ref, xn_out_ref = None, None
        (x_out_ref,) = rest
    ts = x_ref.shape[0] // n_sub
    for s in range(n_sub):
        rows = slice(s * ts, (s + 1) * ts)
        f = None
        for a_ref, w_ref in zip(act_refs, w_refs):
            d = jnp.dot(a_ref[rows, :], w_ref[...], preferred_element_type=F32)
            f = d if f is None else f + d
        x_new = x_ref[rows, :] + scale * _rms(f, gp_ref[...])
        x_out_ref[rows, :] = x_new
        if has_next:
            xn_out_ref[rows, :] = _rms(x_new, gn_ref[...]).astype(xn_out_ref.dtype)


def matmul_post(acts, w_stack, layer, x, g_post, g_next, *, scale, tm, n_sub):
    m, d = x.shape
    tm = _tile(m, tm)
    n_act = len(acts)
    has_next = g_next is not None
    k_each = acts[0].shape[1]
    assert all(a.shape[1] == k_each for a in acts) and w_stack.shape[1] == n_act * k_each
    act_specs = [pl.BlockSpec((tm, k_each), lambda i: (i, 0)) for _ in acts]
    w_specs = [pl.BlockSpec((None, k_each, d), functools.partial(lambda i, r: (layer, r, 0), r=r),
                            pipeline_mode=pl.Buffered(1)) for r in range(n_act)]
    row = pl.BlockSpec((tm, d), lambda i: (i, 0))

    def vec(pair):
        stack, lyr = pair
        return stack.reshape(stack.shape[0], 1, d), pl.BlockSpec((None, 1, d), lambda i: (lyr, 0, 0))

    gp_arr, gp_spec = vec(g_post)
    in_specs = act_specs + w_specs + [row, gp_spec]
    args = list(acts) + [w_stack] * n_act + [x, gp_arr]
    out_shape = [jax.ShapeDtypeStruct((m, d), F32)]
    out_specs = [row]
    if has_next:
        gn_arr, gn_spec = vec(g_next)
        in_specs.append(gn_spec)
        args.append(gn_arr)
        out_shape.append(jax.ShapeDtypeStruct((m, d), BF16))
        out_specs.append(row)
    outs = pl.pallas_call(
        functools.partial(_mm_post_kernel, n_act=n_act, n_sub=n_sub, scale=scale, has_next=has_next),
        grid=(m // tm,),
        in_specs=in_specs,
        out_specs=out_specs,
        out_shape=out_shape,
        compiler_params=_params("parallel"),
        name="mm_post",
    )(*args)
    return (outs[0], outs[1]) if has_next else (outs[0], None)


def _sgu_kernel(u_ref, v_ref, w_ref, bt_ref, lg_ref, lb_ref, o_ref, *, n_chunks):
    c = GM_CHUNK
    width = GM_HEADS * HEAD_DIM
    row = lax.broadcasted_iota(jnp.int32, (c, c), 0)
    col = lax.broadcasted_iota(jnp.int32, (c, c), 1)
    causal = row >= col
    vg = [_gelu(v_ref[h]) for h in range(GM_HEADS)]
    mu = sum(jnp.sum(x, axis=-1, keepdims=True) for x in vg) * (1.0 / width)
    var = sum(jnp.sum(jnp.square(x - mu), axis=-1, keepdims=True) for x in vg) * (1.0 / width)
    rstd = lax.rsqrt(var + NORM_EPS)
    for h in range(GM_HEADS):
        vn = ((vg[h] - mu) * rstd * lg_ref[h] + lb_ref[h]).astype(BF16)
        w = jnp.where(causal, w_ref[h], 0.0).astype(BF16)
        rhs = jnp.concatenate([vn[n * c:(n + 1) * c] for n in range(n_chunks)], axis=1)
        mixed = jnp.dot(w, rhs, preferred_element_type=F32) + bt_ref[:, h:h + 1]
        ug = _gelu(u_ref[h])
        for n in range(n_chunks):
            y = ug[n * c:(n + 1) * c] * mixed[:, n * HEAD_DIM:(n + 1) * HEAD_DIM]
            o_ref[n * c:(n + 1) * c, h * HEAD_DIM:(h + 1) * HEAD_DIM] = y.astype(o_ref.dtype)


def sgu(proj_hm, sm_w, sm_b, ln_g, ln_b, layer, *, tb=256):
    _, m, _ = proj_hm.shape
    tb = _tile(m, tb)
    width = GM_HEADS * HEAD_DIM
    depth = sm_w.shape[0]
    return pl.pallas_call(
        functools.partial(_sgu_kernel, n_chunks=tb // GM_CHUNK),
        grid=(m // tb,),
        in_specs=[pl.BlockSpec((GM_HEADS, tb, HEAD_DIM), lambda i: (_U_GRP // GM_HEADS, i, 0)),
                  pl.BlockSpec((GM_HEADS, tb, HEAD_DIM), lambda i: (_V_GRP // GM_HEADS, i, 0)),
                  pl.BlockSpec((None, GM_HEADS, GM_CHUNK, GM_CHUNK), lambda i: (layer, 0, 0, 0)),
                  pl.BlockSpec((None, GM_CHUNK, GM_HEADS), lambda i: (layer, 0, 0)),
                  pl.BlockSpec((None, GM_HEADS, 1, HEAD_DIM), lambda i: (layer, 0, 0, 0)),
                  pl.BlockSpec((None, GM_HEADS, 1, HEAD_DIM), lambda i: (layer, 0, 0, 0))],
        out_specs=pl.BlockSpec((tb, width), lambda i: (i, 0)),
        out_shape=jax.ShapeDtypeStruct((m, width), BF16),
        compiler_params=_params("parallel"),
        name="sgu",
    )(proj_hm, proj_hm, sm_w, sm_b.transpose(0, 2, 1),
      ln_g.reshape(depth, GM_HEADS, 1, HEAD_DIM), ln_b.reshape(depth, GM_HEADS, 1, HEAD_DIM))


def _gates_kernel(ba_ref, alog_ref, dtb_ref, g_ref, gt_ref, *, n_chunks):
    c = DN_CHUNK
    ba = ba_ref[...]
    grp = lax.broadcasted_iota(jnp.int32, ba.shape, 1) // DN_HEADS
    beta = jax.nn.sigmoid(ba)
    g = -jnp.exp(alog_ref[...]) * jax.nn.softplus(ba + dtb_ref[...])
    row = lax.broadcasted_iota(jnp.int32, (c, c), 0)
    col = lax.broadcasted_iota(jnp.int32, (c, c), 1)
    tri = (row >= col).astype(F32)
    gcum_chunks, last_chunks = [], []
    for n in range(n_chunks):
        gc = jnp.dot(tri, g[n * c:(n + 1) * c], preferred_element_type=F32, precision=lax.Precision.HIGHEST)
        gcum_chunks.append(gc)
        last_chunks.append(jnp.broadcast_to(gc[c - 1:c, :], gc.shape))
    gcum = jnp.concatenate(gcum_chunks, axis=0)
    glast = jnp.concatenate(last_chunks, axis=0)
    out = jnp.where(grp == _BETA, beta,
                    jnp.where(grp == _GCUM, gcum,
                              jnp.where(grp == _EG, jnp.exp(gcum),
                                        jnp.where(grp == _EDEC, jnp.exp(glast - gcum), 0.0))))
    g_ref[...] = out
    for n in range(n_chunks):
        twice_t = jnp.concatenate([gcum_chunks[n], gcum_chunks[n]], axis=0).T
        for h in range(DN_HEADS):
            lane_idx = _GCUM * DN_HEADS + h
            gt_ref[h, n] = twice_t[lane_idx:lane_idx + 1, :]


def gates(ba, a_log, dt_bias, layer, *, tb=256):
    m = ba.shape[0]
    tb = _tile(m, tb)
    depth = a_log.shape[0]

    def lane_row(p):
        z = jnp.zeros((depth, 1, DN_HEADS), F32)
        r = p.reshape(depth, 1, DN_HEADS)
        tail = jnp.zeros((depth, 1, LANES - (1 + _N_DECAY_COPIES) * DN_HEADS), F32)
        return jnp.concatenate([z] + [r] * _N_DECAY_COPIES + [tail], axis=-1)

    return pl.pallas_call(
        functools.partial(_gates_kernel, n_chunks=tb // DN_CHUNK),
        grid=(m // tb,),
        in_specs=[pl.BlockSpec((tb, LANES), lambda i: (i, 0)),
                  pl.BlockSpec((None, 1, LANES), lambda i: (layer, 0, 0)),
                  pl.BlockSpec((None, 1, LANES), lambda i: (layer, 0, 0))],
        out_specs=[pl.BlockSpec((tb, LANES), lambda i: (i, 0)),
                   pl.BlockSpec((DN_HEADS, tb // DN_CHUNK, 1, 2 * DN_CHUNK), lambda i: (0, i, 0, 0))],
        out_shape=[jax.ShapeDtypeStruct((m, LANES), F32),
                   jax.ShapeDtypeStruct((DN_HEADS, m // DN_CHUNK, 1, 2 * DN_CHUNK), F32)],
        compiler_params=_params("parallel"),
        name="dn_gates",
    )(ba, lane_row(a_log), lane_row(dt_bias))


def _dot_nt(a, b):
    return lax.dot_general(a, b, (((1,), (1,)), ((), ())), preferred_element_type=F32)


def _bdot(a, b):
    return lax.dot_general(a, b, (((2,), (1,)), ((0,), (0,))), preferred_element_type=F32)


def _bdot_nt(a, b):
    return lax.dot_general(a, b, (((2,), (2,)), ((0,), (0,))), preferred_element_type=F32)


def _bdot_tn(a, b):
    return lax.dot_general(a, b, (((1,), (1,)), ((0,), (0,))), preferred_element_type=F32)


def _split(x):
    hi = x.astype(BF16)
    lo = (x - hi.astype(F32)).astype(BF16)
    return hi, lo


def _as_lhs(parts):
    hi, lo = parts
    return jnp.concatenate([hi, lo], axis=-1)


def _as_rhs(parts):
    hi, lo = parts
    return jnp.concatenate([hi, lo, hi, lo], axis=-2)


def _unit_lower_inverse2(n2):
    c = n2.shape[-2]
    row = lax.broadcasted_iota(jnp.int32, n2.shape[-2:], 0)
    col = lax.broadcasted_iota(jnp.int32, n2.shape[-2:], 1)
    eye2 = (row == (col % c)).astype(F32)
    inv = eye2 + n2
    parts = _split(n2)
    span = 2
    while span < c:
        n2 = _bdot(_as_lhs(parts), _as_rhs(parts))
        parts = _split(n2)
        inv = inv + _bdot(_as_lhs(_split(inv)), _as_rhs(parts))
        span *= 2
    return inv


def _deltanet_kernel(q_ref, k_ref, v_ref, z_ref, g_ref, gt_ref, cw_ref, nw_ref,
                     o_ref, state_ref, xbuf_ref, *, n_chunks):
    c = DN_CHUNK
    tb = n_chunks * c
    hd = HEAD_DIM
    nh = DN_HEADS
    n_pair = nh * n_chunks

    @pl.when(pl.program_id(1) == 0)
    def _():
        state_ref[...] = jnp.zeros_like(state_ref)
        xbuf_ref[:, 0:8, :] = jnp.zeros((3 * nh, 8, hd), F32)

    xbuf_ref[0:nh, 8:8 + tb, :] = q_ref[...]
    xbuf_ref[nh:2 * nh, 8:8 + tb, :] = k_ref[...]
    xbuf_ref[2 * nh:3 * nh, 8:8 + tb, :] = v_ref[...]
    y = cw_ref[:, 0:1, :] * xbuf_ref[:, 5:5 + tb, :]
    for i in range(1, CONV_WIDTH):
        y = y + cw_ref[:, i:i + 1, :] * xbuf_ref[:, 5 + i:5 + i + tb, :]
    xbuf_ref[:, 0:8, :] = xbuf_ref[:, tb:tb + 8, :]
    y = _silu(y)
    q, k, v = y[0:nh], y[nh:2 * nh], y[2 * nh:3 * nh]
    qn = q * lax.rsqrt(jnp.sum(q * q, axis=-1, keepdims=True) + NORM_EPS) * (hd ** -0.5)
    kn = k * lax.rsqrt(jnp.sum(k * k, axis=-1, keepdims=True) + NORM_EPS)

    gates_blk = g_ref[...]
    lane = lax.broadcasted_iota(jnp.int32, gates_blk.shape, 1)

    def gate_cols(grp):
        return jnp.stack([jnp.sum(jnp.where(lane == grp * nh + h, gates_blk, 0.0), axis=-1, keepdims=True)
                          for h in range(nh)], axis=0)

    beta, gc, eg, edec = (gate_cols(grp) for grp in (_BETA, _GCUM, _EG, _EDEC))

    def pairs(a):
        return a.reshape(n_pair, c, a.shape[-1])

    def unpair(a):
        return a.reshape(nh, n_chunks, c, a.shape[-1])

    row2 = lax.broadcasted_iota(jnp.int32, (c, 2 * c), 0)
    col2 = lax.broadcasted_iota(jnp.int32, (c, 2 * c), 1) % c
    causal2 = row2 >= col2
    strict2 = row2 > col2

    gr2 = gt_ref[...].reshape(n_pair, 1, 2 * c)
    decay2 = jnp.where(causal2, jnp.exp(jnp.where(causal2, pairs(gc) - gr2, 0.0)), 0.0)
    kb = kn * beta
    kn_b = pairs(kn).astype(BF16)
    kkt2 = _bdot_nt(pairs(kb).astype(BF16), jnp.concatenate([kn_b, kn_b], axis=1)) * decay2
    inv2 = _unit_lower_inverse2(jnp.where(strict2, -kkt2, 0.0))
    rhs = pairs(jnp.concatenate([v * beta, kb * eg], axis=-1))
    sol = unpair(_bdot(_as_lhs(_split(inv2)), _as_rhs(_split(rhs))))
    u4, w4 = sol[..., :hd], sol[..., hd:]
    qk = _bdot_nt(pairs(qn).astype(BF16), kn_b) * decay2[..., :c]
    qk4 = unpair(qk)
    qe4 = unpair(pairs(qn * eg))
    kd4 = unpair(pairs(kn * edec))
    eg4 = unpair(pairs(eg))

    state = state_ref[...]
    for n in range(n_chunks):
        state_b = state.astype(BF16)
        ws_qs = _bdot(jnp.concatenate([w4[:, n].astype(BF16), qe4[:, n].astype(BF16)], axis=1), state_b)
        v_new = u4[:, n] - ws_qs[:, :c]
        v_new_b = v_new.astype(BF16)
        o_n = ws_qs[:, c:] + _bdot(qk4[:, n].astype(BF16), v_new_b)
        state = state * eg4[:, n, c - 1:c, :] + _bdot_tn(kd4[:, n].astype(BF16), v_new_b)
        out = _rms(o_n, nw_ref[...]) * _silu(z_ref[:, n * c:(n + 1) * c, :])
        for h in range(nh):
            o_ref[n * c:(n + 1) * c, h * hd:(h + 1) * hd] = out[h].astype(o_ref.dtype)
    state_ref[...] = state


def deltanet(proj_hm, gates_arr, gates_t, conv_w, dn_norm_w, layer, *, batch, tb=128):
    _, m, _ = proj_hm.shape
    t_len = m // batch
    tb = _tile(t_len, tb)
    n_t = t_len // tb
    hd = HEAD_DIM
    depth = conv_w.shape[0]
    cw = conv_w.reshape(depth, CONV_WIDTH, 3 * DN_HEADS, hd).transpose(0, 2, 1, 3)

    def grp(base):
        return pl.BlockSpec((DN_HEADS, tb, hd), lambda b, t: (base // DN_HEADS, b * n_t + t, 0))

    return pl.pallas_call(
        functools.partial(_deltanet_kernel, n_chunks=tb // DN_CHUNK),
        grid=(batch, n_t),
        in_specs=[grp(_Q_GRP), grp(_K_GRP), grp(_VV_GRP), grp(_Z_GRP),
                  pl.BlockSpec((tb, LANES), lambda b, t: (b * n_t + t, 0)),
                  pl.BlockSpec((DN_HEADS, tb // DN_CHUNK, 1, 2 * DN_CHUNK), lambda b, t: (0, b * n_t + t, 0, 0)),
                  pl.BlockSpec((None, 3 * DN_HEADS, CONV_WIDTH, hd), lambda b, t: (layer, 0, 0, 0)),
                  pl.BlockSpec((None, 1, hd), lambda b, t: (layer, 0, 0))],
        out_specs=pl.BlockSpec((tb, DN_HEADS * hd), lambda b, t: (b * n_t + t, 0)),
        out_shape=jax.ShapeDtypeStruct((m, DN_HEADS * hd), BF16),
        scratch_shapes=[pltpu.VMEM((DN_HEADS, hd, hd), F32),
                        pltpu.VMEM((3 * DN_HEADS, tb + 8, hd), F32)],
        compiler_params=_params("parallel", "arbitrary"),
        name="deltanet",
    )(proj_hm, proj_hm, proj_hm, proj_hm, gates_arr, gates_t, cw, dn_norm_w.reshape(depth, 1, hd))


def _xattn_kernel(q_ref, k_ref, v_ref, o_ref, *, head_dim):
    scale = head_dim ** -0.5
    for h in range(XA_HEADS):
        sl = slice(h * head_dim, (h + 1) * head_dim)
        s = _dot_nt(q_ref[:, sl], k_ref[:, sl]) * scale
        s = s - jnp.max(s, axis=-1, keepdims=True)
        e = jnp.exp(s)
        p = e / jnp.sum(e, axis=-1, keepdims=True)
        o = jnp.dot(p.astype(BF16), v_ref[:, sl], preferred_element_type=F32)
        o_ref[:, sl] = o.astype(o_ref.dtype)


def xattn(q, kv, *, batch, tm=512):
    m, d = q.shape
    t_len = m // batch
    n_mem = kv.shape[0] // batch
    tm = _tile(t_len, tm)
    n_t = t_len // tm
    return pl.pallas_call(
        functools.partial(_xattn_kernel, head_dim=d // XA_HEADS),
        grid=(batch, n_t),
        in_specs=[pl.BlockSpec((tm, d), lambda b, t: (b * n_t + t, 0)),
                  pl.BlockSpec((n_mem, d), lambda b, t: (b, 0)),
                  pl.BlockSpec((n_mem, d), lambda b, t: (b, 1))],
        out_specs=pl.BlockSpec((tm, d), lambda b, t: (b * n_t + t, 0)),
        out_shape=jax.ShapeDtypeStruct((m, d), BF16),
        compiler_params=_params("parallel", "parallel"),
        name="xattn",
    )(q, kv, kv)


def kernel(x, mem, ffn1_norm_pre, ffn1_w_gate_up, ffn1_w_down, ffn1_norm_post, mix_norm_pre, w_in, conv_w, a_log, dt_bias, sm_w, sm_b, sm_ln_g, sm_ln_b, dn_norm_w, w_out, mix_norm_post, xa_norm_pre, mem_norm, w_xq, w_xkv, w_xo, xa_norm_post, ffn2_norm_pre, ffn2_w_gate_up, ffn2_w_down, ffn2_norm_post):
    batch, t_len, d = x.shape
    depth = w_in.shape[0]
    d_ff = ffn1_w_down.shape[1]
    n_main = _N_GRP * LANES
    x2 = x.reshape(batch * t_len, d)
    mem2 = mem.reshape(-1, d)

    w_down1_b, w_down2_b = ffn1_w_down.astype(BF16), ffn2_w_down.astype(BF16)
    w_out_b, w_xo_b = w_out.astype(BF16), w_xo.astype(BF16)
    w_beta, w_dec = w_in[:, :, n_main:n_main + DN_HEADS], w_in[:, :, n_main + DN_HEADS:]
    w_small = jnp.concatenate(
        [w_beta] + [w_dec] * _N_DECAY_COPIES
        + [jnp.zeros((depth, d, LANES - (1 + _N_DECAY_COPIES) * DN_HEADS), F32)], axis=-1)

    def ffn(xn, x2, w_gate_up, w_down_b, l, g_post, g_next):
        hmid = matmul_ws(xn, w_gate_up, l, mode="swiglu", n_cols=d_ff, col_offsets=(0, d_ff))
        return matmul_post([hmid], w_down_b, l, x2, g_post, g_next, scale=0.5, tm=256, n_sub=1)

    xn = rmsnorm(x2, ffn1_norm_pre, 0)
    for l in range(depth):
        x2, xn = ffn(xn, x2, ffn1_w_gate_up, w_down1_b, l, (ffn1_norm_post, l), (mix_norm_pre, l))

        proj_hm = matmul_ws(xn, w_in, l, mode="headmajor", out_dtype=F32, n_cols=n_main)
        ba = matmul_ws(xn, w_small, l, out_dtype=F32, tn=LANES)
        y_a = sgu(proj_hm, sm_w, sm_b, sm_ln_g, sm_ln_b, l)
        gates_arr, gates_t = gates(ba, a_log, dt_bias, l)
        y_b = deltanet(proj_hm, gates_arr, gates_t, conv_w, dn_norm_w, l, batch=batch)
        x2, xn = matmul_post([y_a, y_b], w_out_b, l, x2, (mix_norm_post, l), (xa_norm_pre, l),
                             scale=1.0, tm=512, n_sub=2)

        q = matmul_ws(xn, w_xq, l)
        kv = matmul_ws(rmsnorm(mem2, mem_norm, l), w_xkv, l)
        att = xattn(q, kv, batch=batch)
        x2, xn = matmul_post([att], w_xo_b, l, x2, (xa_norm_post, l), (ffn2_norm_pre, l),
                             scale=1.0, tm=512, n_sub=2)

        g_next = (ffn1_norm_pre, l + 1) if l + 1 < depth else None
        x2, xn = ffn(xn, x2, ffn2_w_gate_up, w_down2_b, l, (ffn2_norm_post, l), g_next)
    return x2.reshape(batch, t_len, d)
```

```python
import functools

import jax
import jax.numpy as jnp
from jax import lax
from jax.experimental import pallas as pl
from jax.experimental.pallas import tpu as pltpu

F32 = jnp.float32
BF16 = jnp.bfloat16

NORM_EPS = 1e-6
HEAD_DIM = 128
GM_HEADS = 8
DN_HEADS = 8
GM_CHUNK = 128
DN_CHUNK = 64
CONV_WIDTH = 4
XA_HEADS = 4
LANES = 128
VMEM_LIMIT_BYTES = 58 * 1024 * 1024

_U_GRP, _V_GRP, _Q_GRP, _K_GRP, _VV_GRP, _Z_GRP = 0, 8, 16, 24, 32, 40
_N_GRP = 48
_BETA, _GCUM, _EG, _EDEC = 0, 1, 2, 3
_N_DECAY_COPIES = 3


def _params(*sem):
    return pltpu.CompilerParams(dimension_semantics=sem, vmem_limit_bytes=VMEM_LIMIT_BYTES)


def _tile(dim, want):
    t = min(dim, want)
    assert dim % t == 0, (dim, want)
    return t


def _rms(x, g):
    return x * lax.rsqrt(jnp.mean(x * x, axis=-1, keepdims=True) + NORM_EPS) * g


def _silu(x):
    return x * jax.nn.sigmoid(x)


def _gelu(x):
    return 0.5 * x * (1.0 + lax.erf(x * (2.0 ** -0.5)))


def _rmsnorm_kernel(x_ref, g_ref, o_ref):
    o_ref[...] = _rms(x_ref[...], g_ref[...]).astype(o_ref.dtype)


def rmsnorm(x, g_stack, layer, out_dtype=BF16, tm=512):
    m, d = x.shape
    tm = _tile(m, tm)
    return pl.pallas_call(
        _rmsnorm_kernel,
        grid=(m // tm,),
        in_specs=[pl.BlockSpec((tm, d), lambda i: (i, 0)),
                  pl.BlockSpec((None, 1, d), lambda i: (layer, 0, 0))],
        out_specs=pl.BlockSpec((tm, d), lambda i: (i, 0)),
        out_shape=jax.ShapeDtypeStruct((m, d), out_dtype),
        compiler_params=_params("parallel"),
        name="rmsnorm",
    )(x, g_stack.reshape(g_stack.shape[0], 1, d))


def _mm_ws_kernel(*refs, n_w, mode):
    x_ref = refs[0]
    w_refs = refs[1:1 + n_w]
    o_ref = refs[1 + n_w]
    wb_refs = refs[2 + n_w:]

    @pl.when(pl.program_id(1) == 0)
    def _():
        for w_ref, wb_ref in zip(w_refs, wb_refs):
            wb_ref[...] = w_ref[...].astype(BF16)

    x = x_ref[...]
    ys = [jnp.dot(x, wb_ref[...], preferred_element_type=F32) for wb_ref in wb_refs]
    if mode == "swiglu":
        gate, up = ys
        o_ref[...] = (_silu(gate) * up).astype(o_ref.dtype)
    elif mode == "headmajor":
        y = ys[0]
        for c in range(y.shape[1] // LANES):
            o_ref[c] = y[:, c * LANES:(c + 1) * LANES].astype(o_ref.dtype)
    else:
        o_ref[...] = ys[0].astype(o_ref.dtype)


def matmul_ws(x, w_stack, layer, *, mode="plain", out_dtype=BF16, tm=1024, tn=512, n_cols=None, col_offsets=(0,)):
    m, k = x.shape
    n = w_stack.shape[2] if n_cols is None else n_cols
    tm, tn = _tile(m, tm), _tile(n, tn)
    n_w = len(col_offsets)
    assert all(off % tn == 0 for off in col_offsets)

    def w_spec(off):
        return pl.BlockSpec((None, k, tn), lambda j, i: (layer, 0, j + off // tn))

    if mode == "headmajor":
        out_shape = jax.ShapeDtypeStruct((n // LANES, m, LANES), out_dtype)
        out_spec = pl.BlockSpec((tn // LANES, tm, LANES), lambda j, i: (j, i, 0))
    else:
        out_shape = jax.ShapeDtypeStruct((m, n), out_dtype)
        out_spec = pl.BlockSpec((tm, tn), lambda j, i: (i, j))
    return pl.pallas_call(
        functools.partial(_mm_ws_kernel, n_w=n_w, mode=mode),
        grid=(n // tn, m // tm),
        in_specs=[pl.BlockSpec((tm, k), lambda j, i: (i, 0))] + [w_spec(off) for off in col_offsets],
        out_specs=out_spec,
        out_shape=out_shape,
        scratch_shapes=[pltpu.VMEM((k, tn), BF16) for _ in range(n_w)],
        compiler_params=_params("arbitrary", "arbitrary"),
        name="mm_ws_" + mode,
    )(x, *([w_stack] * n_w))


def _mm_post_kernel(*refs, n_act, n_cast, n_sub, scale, has_next):
    act_refs = refs[:n_act]
    w_ref, x_ref, gp_ref = refs[n_act:n_act + 3]
    rest = refs[n_act + 3:]
    if has_next:
        gn_ref, x_out_ref, xn_out_ref, wb_ref = rest
    else:
        gn_ref, xn_out_ref = None, None
        x_out_ref, wb_ref = rest
    step = pl.program_id(0)
    ck = w_ref.shape[0]
    k_each = act_refs[0].shape[1]

    @pl.when(step < n_cast)
    def _():
        wb_ref[pl.ds(pl.multiple_of(step * ck, ck), ck), :] = w_ref[...].astype(BF16)

    @pl.when(step >= n_cast)
    def _():
        ts = x_ref.shape[0] // n_sub
        for s in range(n_sub):
            rows = slice(s * ts, (s + 1) * ts)
            f = None
            for i, a_ref in enumerate(act_refs):
                d = jnp.dot(a_ref[rows, :], wb_ref[i * k_each:(i + 1) * k_each, :], preferred_element_type=F32)
                f = d if f is None else f + d
            x_new = x_ref[rows, :] + scale * _rms(f, gp_ref[...])
            x_out_ref[rows, :] = x_new
            if has_next:
                xn_out_ref[rows, :] = _rms(x_new, gn_ref[...]).astype(xn_out_ref.dtype)


def matmul_post(acts, w_stack, layer, x, g_post, g_next, *, scale, tm, n_sub, ck=512):
    m, d = x.shape
    tm = _tile(m, tm)
    n_act = len(acts)
    has_next = g_next is not None
    k_each = acts[0].shape[1]
    k_all = w_stack.shape[1]
    assert all(a.shape[1] == k_each for a in acts) and k_all == n_act * k_each
    ck = _tile(k_all, ck)
    n_cast = k_all // ck

    def tile_idx(i):
        return jnp.maximum(i - n_cast, 0)

    act_specs = [pl.BlockSpec((tm, k_each), lambda i: (tile_idx(i), 0)) for _ in acts]
    w_spec = pl.BlockSpec((None, ck, d), lambda i: (layer, jnp.minimum(i, n_cast - 1), 0))
    row = pl.BlockSpec((tm, d), lambda i: (tile_idx(i), 0))

    def vec(pair):
        stack, lyr = pair
        return stack.reshape(stack.shape[0], 1, d), pl.BlockSpec((None, 1, d), lambda i: (lyr, 0, 0))

    gp_arr, gp_spec = vec(g_post)
    in_specs = act_specs + [w_spec, row, gp_spec]
    args = list(acts) + [w_stack, x, gp_arr]
    out_shape = [jax.ShapeDtypeStruct((m, d), F32)]
    out_specs = [row]
    if has_next:
        gn_arr, gn_spec = vec(g_next)
        in_specs.append(gn_spec)
        args.append(gn_arr)
        out_shape.append(jax.ShapeDtypeStruct((m, d), BF16))
        out_specs.append(row)
    outs = pl.pallas_call(
        functools.partial(_mm_post_kernel, n_act=n_act, n_cast=n_cast, n_sub=n_sub, scale=scale,
                          has_next=has_next),
        grid=(n_cast + m // tm,),
        in_specs=in_specs,
        out_specs=out_specs,
        out_shape=out_shape,
        scratch_shapes=[pltpu.VMEM((k_all, d), BF16)],
        compiler_params=_params("arbitrary"),
        name="mm_post",
    )(*args)
    return (outs[0], outs[1]) if has_next else (outs[0], None)


def _sgu_kernel(u_ref, v_ref, w_ref, bt_ref, lg_ref, lb_ref, o_ref, *, n_chunks):
    c = GM_CHUNK
    width = GM_HEADS * HEAD_DIM
    row = lax.broadcasted_iota(jnp.int32, (c, c), 0)
    col = lax.broadcasted_iota(jnp.int32, (c, c), 1)
    causal = row >= col
    vg = [_gelu(v_ref[h].astype(F32)) for h in range(GM_HEADS)]
    mu = sum(jnp.sum(x, axis=-1, keepdims=True) for x in vg) * (1.0 / width)
    var = sum(jnp.sum(jnp.square(x - mu), axis=-1, keepdims=True) for x in vg) * (1.0 / width)
    rstd = lax.rsqrt(var + NORM_EPS)
    for h in range(GM_HEADS):
        vn = ((vg[h] - mu) * rstd * lg_ref[h] + lb_ref[h]).astype(BF16)
        w = jnp.where(causal, w_ref[h], 0.0).astype(BF16)
        rhs = jnp.concatenate([vn[n * c:(n + 1) * c] for n in range(n_chunks)], axis=1)
        mixed = jnp.dot(w, rhs, preferred_element_type=F32) + bt_ref[:, h:h + 1]
        ug = _gelu(u_ref[h].astype(F32))
        for n in range(n_chunks):
            y = ug[n * c:(n + 1) * c] * mixed[:, n * HEAD_DIM:(n + 1) * HEAD_DIM]
            o_ref[n * c:(n + 1) * c, h * HEAD_DIM:(h + 1) * HEAD_DIM] = y.astype(o_ref.dtype)


def sgu(proj_hm, sm_w, sm_b, ln_g, ln_b, layer, *, tb=256):
    _, m, _ = proj_hm.shape
    tb = _tile(m, tb)
    width = GM_HEADS * HEAD_DIM
    depth = sm_w.shape[0]
    return pl.pallas_call(
        functools.partial(_sgu_kernel, n_chunks=tb // GM_CHUNK),
        grid=(m // tb,),
        in_specs=[pl.BlockSpec((GM_HEADS, tb, HEAD_DIM), lambda i: (_U_GRP // GM_HEADS, i, 0)),
                  pl.BlockSpec((GM_HEADS, tb, HEAD_DIM), lambda i: (_V_GRP // GM_HEADS, i, 0)),
                  pl.BlockSpec((None, GM_HEADS, GM_CHUNK, GM_CHUNK), lambda i: (layer, 0, 0, 0)),
                  pl.BlockSpec((None, GM_CHUNK, GM_HEADS), lambda i: (layer, 0, 0)),
                  pl.BlockSpec((None, GM_HEADS, 1, HEAD_DIM), lambda i: (layer, 0, 0, 0)),
                  pl.BlockSpec((None, GM_HEADS, 1, HEAD_DIM), lambda i: (layer, 0, 0, 0))],
        out_specs=pl.BlockSpec((tb, width), lambda i: (i, 0)),
        out_shape=jax.ShapeDtypeStruct((m, width), BF16),
        compiler_params=_params("parallel"),
        name="sgu",
    )(proj_hm, proj_hm, sm_w, sm_b.transpose(0, 2, 1),
      ln_g.reshape(depth, GM_HEADS, 1, HEAD_DIM), ln_b.reshape(depth, GM_HEADS, 1, HEAD_DIM))


def _gates_kernel(x_ref, w_ref, alog_ref, dtb_ref, g_ref, gt_ref, wb_ref, *, n_chunks):
    c = DN_CHUNK

    @pl.when(pl.program_id(0) == 0)
    def _():
        w = w_ref[...]
        lane_w = lax.broadcasted_iota(jnp.int32, w.shape, 1)
        w_valid = jnp.where(lane_w < 2 * DN_HEADS, w, 0.0)
        w_decay = jnp.where(lane_w >= DN_HEADS, w_valid, 0.0)
        w_all = w_valid
        for r in range(1, _N_DECAY_COPIES):
            w_all = w_all + pltpu.roll(w_decay, r * DN_HEADS, axis=1)
        wb_ref[...] = w_all.astype(BF16)

    ba = jnp.dot(x_ref[...], wb_ref[...], preferred_element_type=F32)
    grp = lax.broadcasted_iota(jnp.int32, ba.shape, 1) // DN_HEADS
    beta = jax.nn.sigmoid(ba)
    g = -jnp.exp(alog_ref[...]) * jax.nn.softplus(ba + dtb_ref[...])
    row = lax.broadcasted_iota(jnp.int32, (c, c), 0)
    col = lax.broadcasted_iota(jnp.int32, (c, c), 1)
    tri = (row >= col).astype(F32)
    gcum_chunks, last_chunks = [], []
    for n in range(n_chunks):
        gc = jnp.dot(tri, g[n * c:(n + 1) * c], preferred_element_type=F32, precision=lax.Precision.HIGHEST)
        gcum_chunks.append(gc)
        last_chunks.append(jnp.broadcast_to(gc[c - 1:c, :], gc.shape))
    gcum = jnp.concatenate(gcum_chunks, axis=0)
    glast = jnp.concatenate(last_chunks, axis=0)
    out = jnp.where(grp == _BETA, beta,
                    jnp.where(grp == _GCUM, gcum,
                              jnp.where(grp == _EG, jnp.exp(gcum),
                                        jnp.where(grp == _EDEC, jnp.exp(glast - gcum), 0.0))))
    g_ref[...] = out
    for n in range(n_chunks):
        twice_t = jnp.concatenate([gcum_chunks[n], gcum_chunks[n]], axis=0).T
        for h in range(DN_HEADS):
            lane_idx = _GCUM * DN_HEADS + h
            gt_ref[h, n] = twice_t[lane_idx:lane_idx + 1, :]


def gates(xn, w_in, a_log, dt_bias, layer, *, tb=256):
    m, k = xn.shape
    tb = _tile(m, tb)
    depth = a_log.shape[0]
    tail_block = (w_in.shape[2] - 2 * DN_HEADS) // LANES
    assert tail_block * LANES + 2 * DN_HEADS == w_in.shape[2]

    def lane_row(p):
        z = jnp.zeros((depth, 1, DN_HEADS), F32)
        r = p.reshape(depth, 1, DN_HEADS)
        tail = jnp.zeros((depth, 1, LANES - (1 + _N_DECAY_COPIES) * DN_HEADS), F32)
        return jnp.concatenate([z] + [r] * _N_DECAY_COPIES + [tail], axis=-1)

    return pl.pallas_call(
        functools.partial(_gates_kernel, n_chunks=tb // DN_CHUNK),
        grid=(m // tb,),
        in_specs=[pl.BlockSpec((tb, k), lambda i: (i, 0)),
                  pl.BlockSpec((None, k, LANES), lambda i: (layer, 0, tail_block)),
                  pl.BlockSpec((None, 1, LANES), lambda i: (layer, 0, 0)),
                  pl.BlockSpec((None, 1, LANES), lambda i: (layer, 0, 0))],
        out_specs=[pl.BlockSpec((tb, LANES), lambda i: (i, 0)),
                   pl.BlockSpec((DN_HEADS, tb // DN_CHUNK, 1, 2 * DN_CHUNK), lambda i: (0, i, 0, 0))],
        out_shape=[jax.ShapeDtypeStruct((m, LANES), F32),
                   jax.ShapeDtypeStruct((DN_HEADS, m // DN_CHUNK, 1, 2 * DN_CHUNK), F32)],
        scratch_shapes=[pltpu.VMEM((k, LANES), BF16)],
        compiler_params=_params("arbitrary"),
        name="dn_gates",
    )(xn, w_in, lane_row(a_log), lane_row(dt_bias))


def _dot_nt(a, b):
    return lax.dot_general(a, b, (((1,), (1,)), ((), ())), preferred_element_type=F32)


def _bdot(a, b):
    return lax.dot_general(a, b, (((2,), (1,)), ((0,), (0,))), preferred_element_type=F32)


def _bdot_nt(a, b):
    return lax.dot_general(a, b, (((2,), (2,)), ((0,), (0,))), preferred_element_type=F32)


def _bdot_tn(a, b):
    return lax.dot_general(a, b, (((1,), (1,)), ((0,), (0,))), preferred_element_type=F32)


def _split(x):
    hi = x.astype(BF16)
    lo = (x - hi.astype(F32)).astype(BF16)
    return hi, lo


def _as_lhs(parts):
    hi, lo = parts
    return jnp.concatenate([hi, lo], axis=-1)


def _as_rhs(parts):
    hi, lo = parts
    return jnp.concatenate([hi, lo, hi, lo], axis=-2)


def _unit_lower_inverse2(n2):
    c = n2.shape[-2]
    row = lax.broadcasted_iota(jnp.int32, n2.shape[-2:], 0)
    col = lax.broadcasted_iota(jnp.int32, n2.shape[-2:], 1)
    eye2 = (row == (col % c)).astype(F32)
    inv = eye2 + n2
    parts = _split(n2)
    span = 2
    while span < c:
        n2 = _bdot(_as_lhs(parts), _as_rhs(parts))
        parts = _split(n2)
        inv = inv + _bdot(_as_lhs(_split(inv)), _as_rhs(parts))
        span *= 2
    return inv


def _deltanet_kernel(q_ref, k_ref, v_ref, z_ref, g_ref, gt_ref, cw_ref, nw_ref,
                     o_ref, state_ref, xbuf_ref, *, n_chunks):
    c = DN_CHUNK
    tb = n_chunks * c
    hd = HEAD_DIM
    nh = DN_HEADS
    n_pair = nh * n_chunks

    @pl.when(pl.program_id(1) == 0)
    def _():
        state_ref[...] = jnp.zeros_like(state_ref)
        xbuf_ref[:, 0:8, :] = jnp.zeros((3 * nh, 8, hd), F32)

    xbuf_ref[0:nh, 8:8 + tb, :] = q_ref[...].astype(F32)
    xbuf_ref[nh:2 * nh, 8:8 + tb, :] = k_ref[...].astype(F32)
    xbuf_ref[2 * nh:3 * nh, 8:8 + tb, :] = v_ref[...].astype(F32)
    y = cw_ref[:, 0:1, :] * xbuf_ref[:, 5:5 + tb, :]
    for i in range(1, CONV_WIDTH):
        y = y + cw_ref[:, i:i + 1, :] * xbuf_ref[:, 5 + i:5 + i + tb, :]
    xbuf_ref[:, 0:8, :] = xbuf_ref[:, tb:tb + 8, :]
    y = _silu(y)
    q, k, v = y[0:nh], y[nh:2 * nh], y[2 * nh:3 * nh]
    qn = q * lax.rsqrt(jnp.sum(q * q, axis=-1, keepdims=True) + NORM_EPS) * (hd ** -0.5)
    kn = k * lax.rsqrt(jnp.sum(k * k, axis=-1, keepdims=True) + NORM_EPS)

    gates_blk = g_ref[...]
    lane = lax.broadcasted_iota(jnp.int32, gates_blk.shape, 1)

    def gate_cols(grp):
        return jnp.stack([jnp.sum(jnp.where(lane == grp * nh + h, gates_blk, 0.0), axis=-1, keepdims=True)
                          for h in range(nh)], axis=0)

    beta, gc, eg, edec = (gate_cols(grp) for grp in (_BETA, _GCUM, _EG, _EDEC))

    def pairs(a):
        return a.reshape(n_pair, c, a.shape[-1])

    def unpair(a):
        return a.reshape(nh, n_chunks, c, a.shape[-1])

    row2 = lax.broadcasted_iota(jnp.int32, (c, 2 * c), 0)
    col2 = lax.broadcasted_iota(jnp.int32, (c, 2 * c), 1) % c
    causal2 = row2 >= col2
    strict2 = row2 > col2

    gr2 = gt_ref[...].reshape(n_pair, 1, 2 * c)
    decay2 = jnp.where(causal2, jnp.exp(jnp.where(causal2, pairs(gc) - gr2, 0.0)), 0.0)
    kb = kn * beta
    kn_b = pairs(kn).astype(BF16)
    kkt2 = _bdot_nt(pairs(kb).astype(BF16), jnp.concatenate([kn_b, kn_b], axis=1)) * decay2
    inv2 = _unit_lower_inverse2(jnp.where(strict2, -kkt2, 0.0))
    rhs = pairs(jnp.concatenate([v * beta, kb * eg], axis=-1))
    sol = unpair(_bdot(_as_lhs(_split(inv2)), _as_rhs(_split(rhs))))
    u4, w4 = sol[..., :hd], sol[..., hd:]
    qk = _bdot_nt(pairs(qn).astype(BF16), kn_b) * decay2[..., :c]
    qk4 = unpair(qk)
    qe4 = unpair(pairs(qn * eg))
    kd4 = unpair(pairs(kn * edec))
    eg4 = unpair(pairs(eg))

    state = state_ref[...]
    for n in range(n_chunks):
        state_b = state.astype(BF16)
        ws_qs = _bdot(jnp.concatenate([w4[:, n].astype(BF16), qe4[:, n].astype(BF16)], axis=1), state_b)
        v_new = u4[:, n] - ws_qs[:, :c]
        v_new_b = v_new.astype(BF16)
        o_n = ws_qs[:, c:] + _bdot(qk4[:, n].astype(BF16), v_new_b)
        state = state * eg4[:, n, c - 1:c, :] + _bdot_tn(kd4[:, n].astype(BF16), v_new_b)
        out = _rms(o_n, nw_ref[...]) * _silu(z_ref[:, n * c:(n + 1) * c, :].astype(F32))
        for h in range(nh):
            o_ref[n * c:(n + 1) * c, h * hd:(h + 1) * hd] = out[h].astype(o_ref.dtype)
    state_ref[...] = state


def deltanet(proj_hm, gates_arr, gates_t, conv_w, dn_norm_w, layer, *, batch, tb=256):
    _, m, _ = proj_hm.shape
    t_len = m // batch
    tb = _tile(t_len, tb)
    n_t = t_len // tb
    hd = HEAD_DIM
    depth = conv_w.shape[0]
    cw = conv_w.reshape(depth, CONV_WIDTH, 3 * DN_HEADS, hd).transpose(0, 2, 1, 3)

    def grp(base):
        return pl.BlockSpec((DN_HEADS, tb, hd), lambda b, t: (base // DN_HEADS, b * n_t + t, 0))

    return pl.pallas_call(
        functools.partial(_deltanet_kernel, n_chunks=tb // DN_CHUNK),
        grid=(batch, n_t),
        in_specs=[grp(_Q_GRP), grp(_K_GRP), grp(_VV_GRP), grp(_Z_GRP),
                  pl.BlockSpec((tb, LANES), lambda b, t: (b * n_t + t, 0)),
                  pl.BlockSpec((DN_HEADS, tb // DN_CHUNK, 1, 2 * DN_CHUNK), lambda b, t: (0, b * n_t + t, 0, 0)),
                  pl.BlockSpec((None, 3 * DN_HEADS, CONV_WIDTH, hd), lambda b, t: (layer, 0, 0, 0)),
                  pl.BlockSpec((None, 1, hd), lambda b, t: (layer, 0, 0))],
        out_specs=pl.BlockSpec((tb, DN_HEADS * hd), lambda b, t: (b * n_t + t, 0)),
        out_shape=jax.ShapeDtypeStruct((m, DN_HEADS * hd), BF16),
        scratch_shapes=[pltpu.VMEM((DN_HEADS, hd, hd), F32),
                        pltpu.VMEM((3 * DN_HEADS, tb + 8, hd), F32)],
        compiler_params=_params("parallel", "arbitrary"),
        name="deltanet",
    )(proj_hm, proj_hm, proj_hm, proj_hm, gates_arr, gates_t, cw, dn_norm_w.reshape(depth, 1, hd))


def _xattn_kernel(q_ref, k_ref, v_ref, o_ref, *, head_dim):
    scale = head_dim ** -0.5
    for h in range(XA_HEADS):
        sl = slice(h * head_dim, (h + 1) * head_dim)
        s = _dot_nt(q_ref[:, sl], k_ref[:, sl]) * scale
        s = s - jnp.max(s, axis=-1, keepdims=True)
        e = jnp.exp(s)
        p = e / jnp.sum(e, axis=-1, keepdims=True)
        o = jnp.dot(p.astype(BF16), v_ref[:, sl], preferred_element_type=F32)
        o_ref[:, sl] = o.astype(o_ref.dtype)


def xattn(q, kv, *, batch, tm=512):
    m, d = q.shape
    t_len = m // batch
    n_mem = kv.shape[0] // batch
    tm = _tile(t_len, tm)
    n_t = t_len // tm
    return pl.pallas_call(
        functools.partial(_xattn_kernel, head_dim=d // XA_HEADS),
        grid=(batch, n_t),
        in_specs=[pl.BlockSpec((tm, d), lambda b, t: (b * n_t + t, 0)),
                  pl.BlockSpec((n_mem, d), lambda b, t: (b, 0)),
                  pl.BlockSpec((n_mem, d), lambda b, t: (b, 1))],
        out_specs=pl.BlockSpec((tm, d), lambda b, t: (b * n_t + t, 0)),
        out_shape=jax.ShapeDtypeStruct((m, d), BF16),
        compiler_params=_params("parallel", "parallel"),
        name="xattn",
    )(q, kv, kv)


def kernel(x, mem, ffn1_norm_pre, ffn1_w_gate_up, ffn1_w_down, ffn1_norm_post, mix_norm_pre, w_in, conv_w, a_log, dt_bias, sm_w, sm_b, sm_ln_g, sm_ln_b, dn_norm_w, w_out, mix_norm_post, xa_norm_pre, mem_norm, w_xq, w_xkv, w_xo, xa_norm_post, ffn2_norm_pre, ffn2_w_gate_up, ffn2_w_down, ffn2_norm_post):
    batch, t_len, d = x.shape
    depth = w_in.shape[0]
    d_ff = ffn1_w_down.shape[1]
    n_main = _N_GRP * LANES
    x2 = x.reshape(batch * t_len, d)
    mem2 = mem.reshape(-1, d)

    def ffn(xn, x2, w_gate_up, w_down, l, g_post, g_next):
        hmid = matmul_ws(xn, w_gate_up, l, mode="swiglu", n_cols=d_ff, col_offsets=(0, d_ff), tm=2048)
        return matmul_post([hmid], w_down, l, x2, g_post, g_next, scale=0.5, tm=256, n_sub=2)

    xn = rmsnorm(x2, ffn1_norm_pre, 0)
    for l in range(depth):
        x2, xn = ffn(xn, x2, ffn1_w_gate_up, ffn1_w_down, l, (ffn1_norm_post, l), (mix_norm_pre, l))

        proj_hm = matmul_ws(xn, w_in, l, mode="headmajor", n_cols=n_main, tn=1024)
        y_a = sgu(proj_hm, sm_w, sm_b, sm_ln_g, sm_ln_b, l)
        gates_arr, gates_t = gates(xn, w_in, a_log, dt_bias, l)
        y_b = deltanet(proj_hm, gates_arr, gates_t, conv_w, dn_norm_w, l, batch=batch)
        x2, xn = matmul_post([y_a, y_b], w_out, l, x2, (mix_norm_post, l), (xa_norm_pre, l),
                             scale=1.0, tm=512, n_sub=4)

        q = matmul_ws(xn, w_xq, l, tn=1024)
        kv = matmul_ws(rmsnorm(mem2, mem_norm, l), w_xkv, l)
        att = xattn(q, kv, batch=batch)
        x2, xn = matmul_post([att], w_xo, l, x2, (xa_norm_post, l), (ffn2_norm_pre, l),
                             scale=1.0, tm=512, n_sub=4)

        g_next = (ffn1_norm_pre, l + 1) if l + 1 < depth else None
        x2, xn = ffn(xn, x2, ffn2_w_gate_up, ffn2_w_down, l, (ffn2_norm_post, l), g_next)
    return x2.reshape(batch, t_len, d)
```

```python
import functools

import jax
import jax.numpy as jnp
from jax import lax
from jax.experimental import pallas as pl
from jax.experimental.pallas import tpu as pltpu

F32 = jnp.float32
BF16 = jnp.bfloat16

NORM_EPS = 1e-6
HEAD_DIM = 128
GM_HEADS = 8
DN_HEADS = 8
GM_CHUNK = 128
DN_CHUNK = 64
CONV_WIDTH = 4
XA_HEADS = 4
LANES = 128
VMEM_LIMIT_BYTES = 58 * 1024 * 1024

_U_GRP, _V_GRP, _Q_GRP, _K_GRP, _VV_GRP, _Z_GRP = 0, 8, 16, 24, 32, 40
_N_GRP = 48
_BETA, _GCUM, _EG, _EDEC = 0, 1, 2, 3
_N_DECAY_COPIES = 3


def _params(*sem):
    return pltpu.CompilerParams(dimension_semantics=sem, vmem_limit_bytes=VMEM_LIMIT_BYTES)


def _tile(dim, want):
    t = min(dim, want)
    assert dim % t == 0, (dim, want)
    return t


def _rms(x, g):
    return x * lax.rsqrt(jnp.mean(x * x, axis=-1, keepdims=True) + NORM_EPS) * g


def _silu(x):
    return x * jax.nn.sigmoid(x)


def _gelu(x):
    return 0.5 * x * (1.0 + lax.erf(x * (2.0 ** -0.5)))


def _rmsnorm_kernel(x_ref, g_ref, o_ref):
    o_ref[...] = _rms(x_ref[...], g_ref[...]).astype(o_ref.dtype)


def rmsnorm(x, g_stack, layer, out_dtype=BF16, tm=512):
    m, d = x.shape
    tm = _tile(m, tm)
    return pl.pallas_call(
        _rmsnorm_kernel,
        grid=(m // tm,),
        in_specs=[pl.BlockSpec((tm, d), lambda i: (i, 0)),
                  pl.BlockSpec((None, 1, d), lambda i: (layer, 0, 0))],
        out_specs=pl.BlockSpec((tm, d), lambda i: (i, 0)),
        out_shape=jax.ShapeDtypeStruct((m, d), out_dtype),
        compiler_params=_params("parallel"),
        name="rmsnorm",
    )(x, g_stack.reshape(g_stack.shape[0], 1, d))


_ROWS_PER_DOT = 1024


def _mm_ws_kernel(*refs, n_w, mode, w_rows_are_outputs):
    x_ref = refs[0]
    w_refs = refs[1:1 + n_w]
    o_ref = refs[1 + n_w]
    wb_refs = refs[2 + n_w:]

    @pl.when(pl.program_id(1) == 0)
    def _():
        for w_ref, wb_ref in zip(w_refs, wb_refs):
            w = w_ref[...]
            wb_ref[...] = (w.T if w_rows_are_outputs else w).astype(BF16)

    tm = x_ref.shape[0]
    for r0 in range(0, tm, min(tm, _ROWS_PER_DOT)):
        rows = slice(r0, r0 + min(tm, _ROWS_PER_DOT))
        x = x_ref[rows, :]
        ys = [jnp.dot(x, wb_ref[...], preferred_element_type=F32) for wb_ref in wb_refs]
        if mode == "swiglu":
            gate, up = ys
            o_ref[rows, :] = (_silu(gate) * up).astype(o_ref.dtype)
        elif mode == "headmajor":
            y = ys[0]
            for c in range(y.shape[1] // LANES):
                o_ref[c, rows, :] = y[:, c * LANES:(c + 1) * LANES].astype(o_ref.dtype)
        else:
            o_ref[rows, :] = ys[0].astype(o_ref.dtype)


def matmul_ws(x, w_stack, layer, *, mode="plain", out_dtype=BF16, tm=1024, tn=512, n_cols=None, col_offsets=(0,),
              w_rows_are_outputs=False):
    m, k = x.shape
    n = w_stack.shape[1 if w_rows_are_outputs else 2] if n_cols is None else n_cols
    tm, tn = _tile(m, tm), _tile(n, tn)
    n_w = len(col_offsets)
    assert all(off % tn == 0 for off in col_offsets)

    def w_spec(off):
        if w_rows_are_outputs:
            return pl.BlockSpec((None, tn, k), lambda j, i: (layer, j + off // tn, 0))
        return pl.BlockSpec((None, k, tn), lambda j, i: (layer, 0, j + off // tn))

    if mode == "headmajor":
        out_shape = jax.ShapeDtypeStruct((n // LANES, m, LANES), out_dtype)
        out_spec = pl.BlockSpec((tn // LANES, tm, LANES), lambda j, i: (j, i, 0))
    else:
        out_shape = jax.ShapeDtypeStruct((m, n), out_dtype)
        out_spec = pl.BlockSpec((tm, tn), lambda j, i: (i, j))
    return pl.pallas_call(
        functools.partial(_mm_ws_kernel, n_w=n_w, mode=mode, w_rows_are_outputs=w_rows_are_outputs),
        grid=(n // tn, m // tm),
        in_specs=[pl.BlockSpec((tm, k), lambda j, i: (i, 0))] + [w_spec(off) for off in col_offsets],
        out_specs=out_spec,
        out_shape=out_shape,
        scratch_shapes=[pltpu.VMEM((k, tn), BF16) for _ in range(n_w)],
        compiler_params=_params("arbitrary", "arbitrary"),
        name="mm_ws_" + mode,
    )(x, *([w_stack] * n_w))


def _mm_post_kernel(*refs, n_act, n_cast, n_sub, scale, has_next):
    act_refs = refs[:n_act]
    w_ref, x_ref, gp_ref = refs[n_act:n_act + 3]
    rest = refs[n_act + 3:]
    if has_next:
        gn_ref, x_out_ref, xn_out_ref, wb_ref = rest
    else:
        gn_ref, xn_out_ref = None, None
        x_out_ref, wb_ref = rest
    step = pl.program_id(0)
    ck = w_ref.shape[0]
    k_each = act_refs[0].shape[1]

    @pl.when(step < n_cast)
    def _():
        wb_ref[pl.ds(pl.multiple_of(step * ck, ck), ck), :] = w_ref[...].astype(BF16)

    @pl.when(step >= n_cast)
    def _():
        ts = x_ref.shape[0] // n_sub
        for s in range(n_sub):
            rows = slice(s * ts, (s + 1) * ts)
            f = None
            for i, a_ref in enumerate(act_refs):
                d = jnp.dot(a_ref[rows, :], wb_ref[i * k_each:(i + 1) * k_each, :], preferred_element_type=F32)
                f = d if f is None else f + d
            x_new = x_ref[rows, :] + scale * _rms(f, gp_ref[...])
            x_out_ref[rows, :] = x_new
            if has_next:
                xn_out_ref[rows, :] = _rms(x_new, gn_ref[...]).astype(xn_out_ref.dtype)


def matmul_post(acts, w_stack, layer, x, g_post, g_next, *, scale, tm, n_sub, ck=512):
    m, d = x.shape
    tm = _tile(m, tm)
    n_act = len(acts)
    has_next = g_next is not None
    k_each = acts[0].shape[1]
    k_all = w_stack.shape[1]
    assert all(a.shape[1] == k_each for a in acts) and k_all == n_act * k_each
    ck = _tile(k_all, ck)
    n_cast = k_all // ck

    def tile_idx(i):
        return jnp.maximum(i - n_cast, 0)

    act_specs = [pl.BlockSpec((tm, k_each), lambda i: (tile_idx(i), 0)) for _ in acts]
    w_spec = pl.BlockSpec((None, ck, d), lambda i: (layer, jnp.minimum(i, n_cast - 1), 0))
    row = pl.BlockSpec((tm, d), lambda i: (tile_idx(i), 0))

    def vec(pair):
        stack, lyr = pair
        return stack.reshape(stack.shape[0], 1, d), pl.BlockSpec((None, 1, d), lambda i: (lyr, 0, 0))

    gp_arr, gp_spec = vec(g_post)
    in_specs = act_specs + [w_spec, row, gp_spec]
    args = list(acts) + [w_stack, x, gp_arr]
    out_shape = [jax.ShapeDtypeStruct((m, d), F32)]
    out_specs = [row]
    if has_next:
        gn_arr, gn_spec = vec(g_next)
        in_specs.append(gn_spec)
        args.append(gn_arr)
        out_shape.append(jax.ShapeDtypeStruct((m, d), BF16))
        out_specs.append(row)
    outs = pl.pallas_call(
        functools.partial(_mm_post_kernel, n_act=n_act, n_cast=n_cast, n_sub=n_sub, scale=scale,
                          has_next=has_next),
        grid=(n_cast + m // tm,),
        in_specs=in_specs,
        out_specs=out_specs,
        out_shape=out_shape,
        scratch_shapes=[pltpu.VMEM((k_all, d), BF16)],
        compiler_params=_params("arbitrary"),
        name="mm_post",
    )(*args)
    return (outs[0], outs[1]) if has_next else (outs[0], None)


def _sgu_kernel(u_ref, v_ref, w_ref, bt_ref, lg_ref, lb_ref, o_ref, *, n_chunks):
    c = GM_CHUNK
    width = GM_HEADS * HEAD_DIM
    row = lax.broadcasted_iota(jnp.int32, (c, c), 0)
    col = lax.broadcasted_iota(jnp.int32, (c, c), 1)
    causal = row >= col
    vg = [_gelu(v_ref[h].astype(F32)) for h in range(GM_HEADS)]
    mu = sum(jnp.sum(x, axis=-1, keepdims=True) for x in vg) * (1.0 / width)
    var = sum(jnp.sum(jnp.square(x - mu), axis=-1, keepdims=True) for x in vg) * (1.0 / width)
    rstd = lax.rsqrt(var + NORM_EPS)
    for h in range(GM_HEADS):
        vn = ((vg[h] - mu) * rstd * lg_ref[h] + lb_ref[h]).astype(BF16)
        w = jnp.where(causal, w_ref[h], 0.0).astype(BF16)
        rhs = jnp.concatenate([vn[n * c:(n + 1) * c] for n in range(n_chunks)], axis=1)
        mixed = jnp.dot(w, rhs, preferred_element_type=F32) + bt_ref[:, h:h + 1]
        ug = _gelu(u_ref[h].astype(F32))
        for n in range(n_chunks):
            y = ug[n * c:(n + 1) * c] * mixed[:, n * HEAD_DIM:(n + 1) * HEAD_DIM]
            o_ref[n * c:(n + 1) * c, h * HEAD_DIM:(h + 1) * HEAD_DIM] = y.astype(o_ref.dtype)


def sgu(proj_hm, sm_w, sm_b, ln_g, ln_b, layer, *, tb=256):
    _, m, _ = proj_hm.shape
    tb = _tile(m, tb)
    width = GM_HEADS * HEAD_DIM
    depth = sm_w.shape[0]
    return pl.pallas_call(
        functools.partial(_sgu_kernel, n_chunks=tb // GM_CHUNK),
        grid=(m // tb,),
        in_specs=[pl.BlockSpec((GM_HEADS, tb, HEAD_DIM), lambda i: (_U_GRP // GM_HEADS, i, 0)),
                  pl.BlockSpec((GM_HEADS, tb, HEAD_DIM), lambda i: (_V_GRP // GM_HEADS, i, 0)),
                  pl.BlockSpec((None, GM_HEADS, GM_CHUNK, GM_CHUNK), lambda i: (layer, 0, 0, 0)),
                  pl.BlockSpec((None, GM_CHUNK, GM_HEADS), lambda i: (layer, 0, 0)),
                  pl.BlockSpec((None, GM_HEADS, 1, HEAD_DIM), lambda i: (layer, 0, 0, 0)),
                  pl.BlockSpec((None, GM_HEADS, 1, HEAD_DIM), lambda i: (layer, 0, 0, 0))],
        out_specs=pl.BlockSpec((tb, width), lambda i: (i, 0)),
        out_shape=jax.ShapeDtypeStruct((m, width), BF16),
        compiler_params=_params("parallel"),
        name="sgu",
    )(proj_hm, proj_hm, sm_w, sm_b.transpose(0, 2, 1),
      ln_g.reshape(depth, GM_HEADS, 1, HEAD_DIM), ln_b.reshape(depth, GM_HEADS, 1, HEAD_DIM))


def _gates_kernel(x_ref, w_ref, alog_ref, dtb_ref, g_ref, gt_ref, wb_ref, *, n_chunks):
    c = DN_CHUNK

    @pl.when(pl.program_id(0) == 0)
    def _():
        w = w_ref[...].T
        lane_w = lax.broadcasted_iota(jnp.int32, w.shape, 1)
        w_valid = jnp.where(lane_w < 2 * DN_HEADS, w, 0.0)
        w_decay = jnp.where(lane_w >= DN_HEADS, w_valid, 0.0)
        w_all = w_valid
        for r in range(1, _N_DECAY_COPIES):
            w_all = w_all + pltpu.roll(w_decay, r * DN_HEADS, axis=1)
        wb_ref[...] = w_all.astype(BF16)

    ba = jnp.dot(x_ref[...], wb_ref[...], preferred_element_type=F32)
    grp = lax.broadcasted_iota(jnp.int32, ba.shape, 1) // DN_HEADS
    beta = jax.nn.sigmoid(ba)
    g = -jnp.exp(alog_ref[...]) * jax.nn.softplus(ba + dtb_ref[...])
    row = lax.broadcasted_iota(jnp.int32, (c, c), 0)
    col = lax.broadcasted_iota(jnp.int32, (c, c), 1)
    tri = (row >= col).astype(F32)
    gcum_chunks, last_chunks = [], []
    for n in range(n_chunks):
        gc = jnp.dot(tri, g[n * c:(n + 1) * c], preferred_element_type=F32, precision=lax.Precision.HIGHEST)
        gcum_chunks.append(gc)
        last_chunks.append(jnp.broadcast_to(gc[c - 1:c, :], gc.shape))
    gcum = jnp.concatenate(gcum_chunks, axis=0)
    glast = jnp.concatenate(last_chunks, axis=0)
    out = jnp.where(grp == _BETA, beta,
                    jnp.where(grp == _GCUM, gcum,
                              jnp.where(grp == _EG, jnp.exp(gcum),
                                        jnp.where(grp == _EDEC, jnp.exp(glast - gcum), 0.0))))
    g_ref[...] = out
    for n in range(n_chunks):
        twice_t = jnp.concatenate([gcum_chunks[n], gcum_chunks[n]], axis=0).T
        for h in range(DN_HEADS):
            lane_idx = _GCUM * DN_HEADS + h
            gt_ref[h, n] = twice_t[lane_idx:lane_idx + 1, :]


def gates(xn, w_in_t, a_log, dt_bias, layer, *, tb=256):
    m, k = xn.shape
    tb = _tile(m, tb)
    depth = a_log.shape[0]
    tail_block = (w_in_t.shape[1] - 2 * DN_HEADS) // LANES
    assert tail_block * LANES + 2 * DN_HEADS == w_in_t.shape[1]

    def lane_row(p):
        z = jnp.zeros((depth, 1, DN_HEADS), F32)
        r = p.reshape(depth, 1, DN_HEADS)
        tail = jnp.zeros((depth, 1, LANES - (1 + _N_DECAY_COPIES) * DN_HEADS), F32)
        return jnp.concatenate([z] + [r] * _N_DECAY_COPIES + [tail], axis=-1)

    return pl.pallas_call(
        functools.partial(_gates_kernel, n_chunks=tb // DN_CHUNK),
        grid=(m // tb,),
        in_specs=[pl.BlockSpec((tb, k), lambda i: (i, 0)),
                  pl.BlockSpec((None, LANES, k), lambda i: (layer, tail_block, 0)),
                  pl.BlockSpec((None, 1, LANES), lambda i: (layer, 0, 0)),
                  pl.BlockSpec((None, 1, LANES), lambda i: (layer, 0, 0))],
        out_specs=[pl.BlockSpec((tb, LANES), lambda i: (i, 0)),
                   pl.BlockSpec((DN_HEADS, tb // DN_CHUNK, 1, 2 * DN_CHUNK), lambda i: (0, i, 0, 0))],
        out_shape=[jax.ShapeDtypeStruct((m, LANES), F32),
                   jax.ShapeDtypeStruct((DN_HEADS, m // DN_CHUNK, 1, 2 * DN_CHUNK), F32)],
        scratch_shapes=[pltpu.VMEM((k, LANES), BF16)],
        compiler_params=_params("arbitrary"),
        name="dn_gates",
    )(xn, w_in_t, lane_row(a_log), lane_row(dt_bias))


def _dot_nt(a, b):
    return lax.dot_general(a, b, (((1,), (1,)), ((), ())), preferred_element_type=F32)


def _bdot(a, b):
    return lax.dot_general(a, b, (((2,), (1,)), ((0,), (0,))), preferred_element_type=F32)


def _bdot_nt(a, b):
    return lax.dot_general(a, b, (((2,), (2,)), ((0,), (0,))), preferred_element_type=F32)


def _bdot_tn(a, b):
    return lax.dot_general(a, b, (((1,), (1,)), ((0,), (0,))), preferred_element_type=F32)


def _split(x):
    hi = x.astype(BF16)
    lo = (x - hi.astype(F32)).astype(BF16)
    return hi, lo


def _as_lhs(parts):
    hi, lo = parts
    return jnp.concatenate([hi, lo], axis=-1)


def _as_rhs(parts):
    hi, lo = parts
    return jnp.concatenate([hi, lo, hi, lo], axis=-2)


def _unit_lower_inverse2(n2):
    c = n2.shape[-2]
    row = lax.broadcasted_iota(jnp.int32, n2.shape[-2:], 0)
    col = lax.broadcasted_iota(jnp.int32, n2.shape[-2:], 1)
    eye2 = (row == (col % c)).astype(F32)
    inv = eye2 + n2
    parts = _split(n2)
    span = 2
    while span < c:
        n2 = _bdot(_as_lhs(parts), _as_rhs(parts))
        parts = _split(n2)
        inv = inv + _bdot(_as_lhs(_split(inv)), _as_rhs(parts))
        span *= 2
    return inv


def _deltanet_kernel(q_ref, k_ref, v_ref, z_ref, g_ref, gt_ref, cw_ref, nw_ref,
                     o_ref, state_ref, xbuf_ref, *, n_chunks):
    c = DN_CHUNK
    tb = n_chunks * c
    hd = HEAD_DIM
    nh = DN_HEADS
    n_pair = nh * n_chunks

    @pl.when(pl.program_id(1) == 0)
    def _():
        state_ref[...] = jnp.zeros_like(state_ref)
        xbuf_ref[:, 0:8, :] = jnp.zeros((3 * nh, 8, hd), F32)

    xbuf_ref[0:nh, 8:8 + tb, :] = q_ref[...].astype(F32)
    xbuf_ref[nh:2 * nh, 8:8 + tb, :] = k_ref[...].astype(F32)
    xbuf_ref[2 * nh:3 * nh, 8:8 + tb, :] = v_ref[...].astype(F32)
    y = cw_ref[:, 0:1, :] * xbuf_ref[:, 5:5 + tb, :]
    for i in range(1, CONV_WIDTH):
        y = y + cw_ref[:, i:i + 1, :] * xbuf_ref[:, 5 + i:5 + i + tb, :]
    xbuf_ref[:, 0:8, :] = xbuf_ref[:, tb:tb + 8, :]
    y = _silu(y)
    q, k, v = y[0:nh], y[nh:2 * nh], y[2 * nh:3 * nh]
    qn = q * lax.rsqrt(jnp.sum(q * q, axis=-1, keepdims=True) + NORM_EPS) * (hd ** -0.5)
    kn = k * lax.rsqrt(jnp.sum(k * k, axis=-1, keepdims=True) + NORM_EPS)

    gates_blk = g_ref[...]
    lane = lax.broadcasted_iota(jnp.int32, gates_blk.shape, 1)

    def gate_cols(grp):
        return jnp.stack([jnp.sum(jnp.where(lane == grp * nh + h, gates_blk, 0.0), axis=-1, keepdims=True)
                          for h in range(nh)], axis=0)

    beta, gc, eg, edec = (gate_cols(grp) for grp in (_BETA, _GCUM, _EG, _EDEC))

    def pairs(a):
        return a.reshape(n_pair, c, a.shape[-1])

    def unpair(a):
        return a.reshape(nh, n_chunks, c, a.shape[-1])

    row2 = lax.broadcasted_iota(jnp.int32, (c, 2 * c), 0)
    col2 = lax.broadcasted_iota(jnp.int32, (c, 2 * c), 1) % c
    causal2 = row2 >= col2
    strict2 = row2 > col2

    gr2 = gt_ref[...].reshape(n_pair, 1, 2 * c)
    decay2 = jnp.where(causal2, jnp.exp(jnp.where(causal2, pairs(gc) - gr2, 0.0)), 0.0)
    kb = kn * beta
    kn_b = pairs(kn).astype(BF16)
    kkt2 = _bdot_nt(pairs(kb).astype(BF16), jnp.concatenate([kn_b, kn_b], axis=1)) * decay2
    inv2 = _unit_lower_inverse2(jnp.where(strict2, -kkt2, 0.0))
    rhs = pairs(jnp.concatenate([v * beta, kb * eg], axis=-1))
    sol = unpair(_bdot(_as_lhs(_split(inv2)), _as_rhs(_split(rhs))))
    u4, w4 = sol[..., :hd], sol[..., hd:]
    qk = _bdot_nt(pairs(qn).astype(BF16), kn_b) * decay2[..., :c]
    qk4 = unpair(qk)
    qe4 = unpair(pairs(qn * eg))
    kd4 = unpair(pairs(kn * edec))
    eg4 = unpair(pairs(eg))

    state = state_ref[...]
    for n in range(n_chunks):
        state_b = state.astype(BF16)
        ws_qs = _bdot(jnp.concatenate([w4[:, n].astype(BF16), qe4[:, n].astype(BF16)], axis=1), state_b)
        v_new = u4[:, n] - ws_qs[:, :c]
        v_new_b = v_new.astype(BF16)
        o_n = ws_qs[:, c:] + _bdot(qk4[:, n].astype(BF16), v_new_b)
        state = state * eg4[:, n, c - 1:c, :] + _bdot_tn(kd4[:, n].astype(BF16), v_new_b)
        out = _rms(o_n, nw_ref[...]) * _silu(z_ref[:, n * c:(n + 1) * c, :].astype(F32))
        for h in range(nh):
            o_ref[n * c:(n + 1) * c, h * hd:(h + 1) * hd] = out[h].astype(o_ref.dtype)
    state_ref[...] = state


def deltanet(proj_hm, gates_arr, gates_t, conv_w, dn_norm_w, layer, *, batch, tb=256):
    _, m, _ = proj_hm.shape
    t_len = m // batch
    tb = _tile(t_len, tb)
    n_t = t_len // tb
    hd = HEAD_DIM
    depth = conv_w.shape[0]
    cw = conv_w.reshape(depth, CONV_WIDTH, 3 * DN_HEADS, hd).transpose(0, 2, 1, 3)

    def grp(base):
        return pl.BlockSpec((DN_HEADS, tb, hd), lambda b, t: (base // DN_HEADS, b * n_t + t, 0))

    return pl.pallas_call(
        functools.partial(_deltanet_kernel, n_chunks=tb // DN_CHUNK),
        grid=(batch, n_t),
        in_specs=[grp(_Q_GRP), grp(_K_GRP), grp(_VV_GRP), grp(_Z_GRP),
                  pl.BlockSpec((tb, LANES), lambda b, t: (b * n_t + t, 0)),
                  pl.BlockSpec((DN_HEADS, tb // DN_CHUNK, 1, 2 * DN_CHUNK), lambda b, t: (0, b * n_t + t, 0, 0)),
                  pl.BlockSpec((None, 3 * DN_HEADS, CONV_WIDTH, hd), lambda b, t: (layer, 0, 0, 0)),
                  pl.BlockSpec((None, 1, hd), lambda b, t: (layer, 0, 0))],
        out_specs=pl.BlockSpec((tb, DN_HEADS * hd), lambda b, t: (b * n_t + t, 0)),
        out_shape=jax.ShapeDtypeStruct((m, DN_HEADS * hd), BF16),
        scratch_shapes=[pltpu.VMEM((DN_HEADS, hd, hd), F32),
                        pltpu.VMEM((3 * DN_HEADS, tb + 8, hd), F32)],
        compiler_params=_params("parallel", "arbitrary"),
        name="deltanet",
    )(proj_hm, proj_hm, proj_hm, proj_hm, gates_arr, gates_t, cw, dn_norm_w.reshape(depth, 1, hd))


def _xattn_kernel(q_ref, k_ref, v_ref, o_ref, *, head_dim):
    scale = head_dim ** -0.5
    for h in range(XA_HEADS):
        sl = slice(h * head_dim, (h + 1) * head_dim)
        s = _dot_nt(q_ref[:, sl], k_ref[:, sl]) * scale
        s = s - jnp.max(s, axis=-1, keepdims=True)
        e = jnp.exp(s)
        p = e / jnp.sum(e, axis=-1, keepdims=True)
        o = jnp.dot(p.astype(BF16), v_ref[:, sl], preferred_element_type=F32)
        o_ref[:, sl] = o.astype(o_ref.dtype)


def xattn(q, kv, *, batch, tm=512):
    m, d = q.shape
    t_len = m // batch
    n_mem = kv.shape[0] // batch
    tm = _tile(t_len, tm)
    n_t = t_len // tm
    return pl.pallas_call(
        functools.partial(_xattn_kernel, head_dim=d // XA_HEADS),
        grid=(batch, n_t),
        in_specs=[pl.BlockSpec((tm, d), lambda b, t: (b * n_t + t, 0)),
                  pl.BlockSpec((n_mem, d), lambda b, t: (b, 0)),
                  pl.BlockSpec((n_mem, d), lambda b, t: (b, 1))],
        out_specs=pl.BlockSpec((tm, d), lambda b, t: (b * n_t + t, 0)),
        out_shape=jax.ShapeDtypeStruct((m, d), BF16),
        compiler_params=_params("parallel", "parallel"),
        name="xattn",
    )(q, kv, kv)


def kernel(x, mem, ffn1_norm_pre, ffn1_w_gate_up, ffn1_w_down, ffn1_norm_post, mix_norm_pre, w_in, conv_w, a_log, dt_bias, sm_w, sm_b, sm_ln_g, sm_ln_b, dn_norm_w, w_out, mix_norm_post, xa_norm_pre, mem_norm, w_xq, w_xkv, w_xo, xa_norm_post, ffn2_norm_pre, ffn2_w_gate_up, ffn2_w_down, ffn2_norm_post):
    batch, t_len, d = x.shape
    depth = w_in.shape[0]
    d_ff = ffn1_w_down.shape[1]
    n_main = _N_GRP * LANES
    x2 = x.reshape(batch * t_len, d)
    mem2 = mem.reshape(-1, d)
    w_in_t = jnp.swapaxes(w_in, 1, 2)

    def ffn(xn, x2, w_gate_up, w_down, l, g_post, g_next):
        hmid = matmul_ws(xn, w_gate_up, l, mode="swiglu", n_cols=d_ff, col_offsets=(0, d_ff), tm=2048)
        return matmul_post([hmid], w_down, l, x2, g_post, g_next, scale=0.5, tm=256, n_sub=2)

    xn = rmsnorm(x2, ffn1_norm_pre, 0)
    for l in range(depth):
        x2, xn = ffn(xn, x2, ffn1_w_gate_up, ffn1_w_down, l, (ffn1_norm_post, l), (mix_norm_pre, l))

        proj_hm = matmul_ws(xn, w_in_t, l, mode="headmajor", n_cols=n_main, tm=2048, tn=1024,
                            w_rows_are_outputs=True)
        y_a = sgu(proj_hm, sm_w, sm_b, sm_ln_g, sm_ln_b, l)
        gates_arr, gates_t = gates(xn, w_in_t, a_log, dt_bias, l)
        y_b = deltanet(proj_hm, gates_arr, gates_t, conv_w, dn_norm_w, l, batch=batch)
        x2, xn = matmul_post([y_a, y_b], w_out, l, x2, (mix_norm_post, l), (xa_norm_pre, l),
                             scale=1.0, tm=512, n_sub=4)

        q = matmul_ws(xn, w_xq, l, tm=2048, tn=1024)
        kv = matmul_ws(rmsnorm(mem2, mem_norm, l), w_xkv, l)
        att = xattn(q, kv, batch=batch)
        x2, xn = matmul_post([att], w_xo, l, x2, (xa_norm_post, l), (ffn2_norm_pre, l),
                             scale=1.0, tm=512, n_sub=4)

        g_next = (ffn1_norm_pre, l + 1) if l + 1 < depth else None
        x2, xn = ffn(xn, x2, ffn2_w_gate_up, ffn2_w_down, l, (ffn2_norm_post, l), g_next)
    return x2.reshape(batch, t_len, d)
```

```python
import functools

import jax
import jax.numpy as jnp
from jax import lax
from jax.experimental import pallas as pl
from jax.experimental.pallas import tpu as pltpu

F32 = jnp.float32
BF16 = jnp.bfloat16

NORM_EPS = 1e-6
HEAD_DIM = 128
GM_HEADS = 8
DN_HEADS = 8
GM_CHUNK = 128
DN_CHUNK = 64
CONV_WIDTH = 4
XA_HEADS = 4
LANES = 128
VMEM_LIMIT_BYTES = 58 * 1024 * 1024

_U_GRP, _V_GRP, _Q_GRP, _K_GRP, _VV_GRP, _Z_GRP = 0, 8, 16, 24, 32, 40
_N_GRP = 48
_BETA, _GCUM, _EG, _EDEC = 0, 1, 2, 3
_N_DECAY_COPIES = 3


def _params(*sem):
    return pltpu.CompilerParams(dimension_semantics=sem, vmem_limit_bytes=VMEM_LIMIT_BYTES)


def _tile(dim, want):
    t = min(dim, want)
    assert dim % t == 0, (dim, want)
    return t


def _rms(x, g):
    return x * lax.rsqrt(jnp.mean(x * x, axis=-1, keepdims=True) + NORM_EPS) * g


def _silu(x):
    return x * jax.nn.sigmoid(x)


def _gelu(x):
    return 0.5 * x * (1.0 + lax.erf(x * (2.0 ** -0.5)))


def _rmsnorm_kernel(x_ref, g_ref, o_ref):
    o_ref[...] = _rms(x_ref[...], g_ref[...]).astype(o_ref.dtype)


def rmsnorm(x, g_stack, layer, out_dtype=BF16, tm=512):
    m, d = x.shape
    tm = _tile(m, tm)
    return pl.pallas_call(
        _rmsnorm_kernel,
        grid=(m // tm,),
        in_specs=[pl.BlockSpec((tm, d), lambda i: (i, 0)),
                  pl.BlockSpec((None, 1, d), lambda i: (layer, 0, 0))],
        out_specs=pl.BlockSpec((tm, d), lambda i: (i, 0)),
        out_shape=jax.ShapeDtypeStruct((m, d), out_dtype),
        compiler_params=_params("parallel"),
        name="rmsnorm",
    )(x, g_stack.reshape(g_stack.shape[0], 1, d))


_ROWS_PER_DOT = 1024


def _mm_ws_kernel(*refs, n_w, mode, w_rows_are_outputs, has_side_cast):
    x_ref = refs[0]
    w_refs = refs[1:1 + n_w]
    if has_side_cast:
        side_ref, o_ref, side_out_ref, wb_ref = refs[1 + n_w:]
    else:
        o_ref, wb_ref = refs[1 + n_w:]
    tn = wb_ref.shape[1] // n_w

    @pl.when(pl.program_id(1) == 0)
    def _():
        for i, w_ref in enumerate(w_refs):
            w = w_ref[...]
            wb_ref[:, i * tn:(i + 1) * tn] = (w.T if w_rows_are_outputs else w).astype(BF16)
        if has_side_cast:
            side_out_ref[...] = side_ref[...].astype(BF16)

    tm = x_ref.shape[0]
    for r0 in range(0, tm, min(tm, _ROWS_PER_DOT)):
        rows = slice(r0, r0 + min(tm, _ROWS_PER_DOT))
        y = jnp.dot(x_ref[rows, :], wb_ref[...], preferred_element_type=F32)
        if mode == "swiglu":
            o_ref[rows, :] = (_silu(y[:, :tn]) * y[:, tn:]).astype(o_ref.dtype)
        elif mode == "headmajor":
            for c in range(tn // LANES):
                o_ref[c, rows, :] = y[:, c * LANES:(c + 1) * LANES].astype(o_ref.dtype)
        else:
            o_ref[rows, :] = y.astype(o_ref.dtype)


def matmul_ws(x, w_stack, layer, *, mode="plain", out_dtype=BF16, tm=1024, tn=512, n_cols=None, col_offsets=(0,),
              w_rows_are_outputs=False, side_cast=None):
    m, k = x.shape
    n = w_stack.shape[1 if w_rows_are_outputs else 2] if n_cols is None else n_cols
    tm, tn = _tile(m, tm), _tile(n, tn)
    n_w = len(col_offsets)
    assert all(off % tn == 0 for off in col_offsets)
    n_col_tiles = n // tn

    def w_spec(off):
        if w_rows_are_outputs:
            return pl.BlockSpec((None, tn, k), lambda j, i: (layer, j + off // tn, 0))
        return pl.BlockSpec((None, k, tn), lambda j, i: (layer, 0, j + off // tn))

    if mode == "headmajor":
        out_shape = jax.ShapeDtypeStruct((n // LANES, m, LANES), out_dtype)
        out_spec = pl.BlockSpec((tn // LANES, tm, LANES), lambda j, i: (j, i, 0))
    else:
        out_shape = jax.ShapeDtypeStruct((m, n), out_dtype)
        out_spec = pl.BlockSpec((tm, tn), lambda j, i: (i, j))
    in_specs = [pl.BlockSpec((tm, k), lambda j, i: (i, 0))] + [w_spec(off) for off in col_offsets]
    args = [x] + [w_stack] * n_w
    if side_cast is not None:
        _, side_rows, side_cols = side_cast.shape
        chunk = side_rows // n_col_tiles
        assert chunk * n_col_tiles == side_rows
        in_specs.append(pl.BlockSpec((None, chunk, side_cols), lambda j, i: (layer, j, 0)))
        args.append(side_cast)
        out_spec = [out_spec, pl.BlockSpec((chunk, side_cols), lambda j, i: (j, 0))]
        out_shape = [out_shape, jax.ShapeDtypeStruct((side_rows, side_cols), BF16)]
    return pl.pallas_call(
        functools.partial(_mm_ws_kernel, n_w=n_w, mode=mode, w_rows_are_outputs=w_rows_are_outputs,
                          has_side_cast=side_cast is not None),
        grid=(n_col_tiles, m // tm),
        in_specs=in_specs,
        out_specs=out_spec,
        out_shape=out_shape,
        scratch_shapes=[pltpu.VMEM((k, n_w * tn), BF16)],
        compiler_params=_params("arbitrary", "arbitrary"),
        name="mm_ws_" + mode,
    )(*args)


def _mm_post_kernel(*refs, n_act, n_cast, n_sub, scale, has_next):
    act_refs = refs[:n_act]
    w_ref, x_ref, gp_ref = refs[n_act:n_act + 3]
    rest = refs[n_act + 3:]
    gn_ref = xn_out_ref = None
    if has_next:
        gn_ref, x_out_ref, xn_out_ref = rest[:3]
        rest = rest[3:]
    else:
        x_out_ref = rest[0]
        rest = rest[1:]
    step = pl.program_id(0)
    k_each = act_refs[0].shape[1]

    if n_cast:
        (wb_ref,) = rest
        ck = w_ref.shape[0]

        @pl.when(step < n_cast)
        def _():
            wb_ref[pl.ds(pl.multiple_of(step * ck, ck), ck), :] = w_ref[...].astype(BF16)
    else:
        wb_ref = w_ref

    @pl.when(step >= n_cast)
    def _():
        ts = x_ref.shape[0] // n_sub
        for s in range(n_sub):
            rows = slice(s * ts, (s + 1) * ts)
            f = None
            for i, a_ref in enumerate(act_refs):
                d = jnp.dot(a_ref[rows, :], wb_ref[i * k_each:(i + 1) * k_each, :], preferred_element_type=F32)
                f = d if f is None else f + d
            x_new = x_ref[rows, :] + scale * _rms(f, gp_ref[...])
            x_out_ref[rows, :] = x_new
            if has_next:
                xn_out_ref[rows, :] = _rms(x_new, gn_ref[...]).astype(xn_out_ref.dtype)


def matmul_post(acts, w_stack, layer, x, g_post, g_next, *, scale, tm, n_sub, ck=512):
    m, d = x.shape
    tm = _tile(m, tm)
    n_act = len(acts)
    has_next = g_next is not None
    k_each = acts[0].shape[1]
    pre_cast = w_stack.ndim == 2
    k_all = w_stack.shape[-2]
    assert all(a.shape[1] == k_each for a in acts) and k_all == n_act * k_each
    if pre_cast:
        assert w_stack.dtype == BF16
        n_cast = 0
        w_spec = pl.BlockSpec((k_all, d), lambda i: (0, 0), pipeline_mode=pl.Buffered(1))
        scratch = []
    else:
        ck = _tile(k_all, ck)
        n_cast = k_all // ck
        w_spec = pl.BlockSpec((None, ck, d), lambda i: (layer, jnp.minimum(i, n_cast - 1), 0))
        scratch = [pltpu.VMEM((k_all, d), BF16)]

    def tile_idx(i):
        return jnp.maximum(i - n_cast, 0)

    act_specs = [pl.BlockSpec((tm, k_each), lambda i: (tile_idx(i), 0)) for _ in acts]
    row = pl.BlockSpec((tm, d), lambda i: (tile_idx(i), 0))

    def vec(pair):
        stack, lyr = pair
        return stack.reshape(stack.shape[0], 1, d), pl.BlockSpec((None, 1, d), lambda i: (lyr, 0, 0))

    gp_arr, gp_spec = vec(g_post)
    in_specs = act_specs + [w_spec, row, gp_spec]
    args = list(acts) + [w_stack, x, gp_arr]
    out_shape = [jax.ShapeDtypeStruct((m, d), F32)]
    out_specs = [row]
    if has_next:
        gn_arr, gn_spec = vec(g_next)
        in_specs.append(gn_spec)
        args.append(gn_arr)
        out_shape.append(jax.ShapeDtypeStruct((m, d), BF16))
        out_specs.append(row)
    outs = pl.pallas_call(
        functools.partial(_mm_post_kernel, n_act=n_act, n_cast=n_cast, n_sub=n_sub, scale=scale,
                          has_next=has_next),
        grid=(n_cast + m // tm,),
        in_specs=in_specs,
        out_specs=out_specs,
        out_shape=out_shape,
        scratch_shapes=scratch,
        compiler_params=_params("arbitrary"),
        name="mm_post",
    )(*args)
    return (outs[0], outs[1]) if has_next else (outs[0], None)


def _sgu_kernel(u_ref, v_ref, w_ref, bt_ref, lg_ref, lb_ref, o_ref, *, n_chunks):
    c = GM_CHUNK
    width = GM_HEADS * HEAD_DIM
    row = lax.broadcasted_iota(jnp.int32, (c, c), 0)
    col = lax.broadcasted_iota(jnp.int32, (c, c), 1)
    causal = row >= col
    vg = [_gelu(v_ref[h].astype(F32)) for h in range(GM_HEADS)]
    mu = sum(jnp.sum(x, axis=-1, keepdims=True) for x in vg) * (1.0 / width)
    var = sum(jnp.sum(jnp.square(x - mu), axis=-1, keepdims=True) for x in vg) * (1.0 / width)
    rstd = lax.rsqrt(var + NORM_EPS)
    for h in range(GM_HEADS):
        vn = ((vg[h] - mu) * rstd * lg_ref[h] + lb_ref[h]).astype(BF16)
        w = jnp.where(causal, w_ref[h], 0.0).astype(BF16)
        rhs = jnp.concatenate([vn[n * c:(n + 1) * c] for n in range(n_chunks)], axis=1)
        mixed = jnp.dot(w, rhs, preferred_element_type=F32) + bt_ref[:, h:h + 1]
        ug = _gelu(u_ref[h].astype(F32))
        for n in range(n_chunks):
            y = ug[n * c:(n + 1) * c] * mixed[:, n * HEAD_DIM:(n + 1) * HEAD_DIM]
            o_ref[n * c:(n + 1) * c, h * HEAD_DIM:(h + 1) * HEAD_DIM] = y.astype(o_ref.dtype)


def sgu(proj_hm, sm_w, sm_b, ln_g, ln_b, layer, *, tb=256):
    _, m, _ = proj_hm.shape
    tb = _tile(m, tb)
    width = GM_HEADS * HEAD_DIM
    depth = sm_w.shape[0]
    return pl.pallas_call(
        functools.partial(_sgu_kernel, n_chunks=tb // GM_CHUNK),
        grid=(m // tb,),
        in_specs=[pl.BlockSpec((GM_HEADS, tb, HEAD_DIM), lambda i: (_U_GRP // GM_HEADS, i, 0)),
                  pl.BlockSpec((GM_HEADS, tb, HEAD_DIM), lambda i: (_V_GRP // GM_HEADS, i, 0)),
                  pl.BlockSpec((None, GM_HEADS, GM_CHUNK, GM_CHUNK), lambda i: (layer, 0, 0, 0)),
                  pl.BlockSpec((None, GM_CHUNK, GM_HEADS), lambda i: (layer, 0, 0)),
                  pl.BlockSpec((None, GM_HEADS, 1, HEAD_DIM), lambda i: (layer, 0, 0, 0)),
                  pl.BlockSpec((None, GM_HEADS, 1, HEAD_DIM), lambda i: (layer, 0, 0, 0))],
        out_specs=pl.BlockSpec((tb, width), lambda i: (i, 0)),
        out_shape=jax.ShapeDtypeStruct((m, width), BF16),
        compiler_params=_params("parallel"),
        name="sgu",
    )(proj_hm, proj_hm, sm_w, sm_b.transpose(0, 2, 1),
      ln_g.reshape(depth, GM_HEADS, 1, HEAD_DIM), ln_b.reshape(depth, GM_HEADS, 1, HEAD_DIM))


def _gates_kernel(x_ref, w_ref, alog_ref, dtb_ref, g_ref, gt_ref, wb_ref, *, n_chunks):
    c = DN_CHUNK

    @pl.when(pl.program_id(0) == 0)
    def _():
        w = w_ref[...].T
        lane_w = lax.broadcasted_iota(jnp.int32, w.shape, 1)
        w_valid = jnp.where(lane_w < 2 * DN_HEADS, w, 0.0)
        w_decay = jnp.where(lane_w >= DN_HEADS, w_valid, 0.0)
        w_all = w_valid
        for r in range(1, _N_DECAY_COPIES):
            w_all = w_all + pltpu.roll(w_decay, r * DN_HEADS, axis=1)
        wb_ref[...] = w_all.astype(BF16)

    ba = jnp.dot(x_ref[...], wb_ref[...], preferred_element_type=F32)
    grp = lax.broadcasted_iota(jnp.int32, ba.shape, 1) // DN_HEADS
    beta = jax.nn.sigmoid(ba)
    g = -jnp.exp(alog_ref[...]) * jax.nn.softplus(ba + dtb_ref[...])
    row = lax.broadcasted_iota(jnp.int32, (c, c), 0)
    col = lax.broadcasted_iota(jnp.int32, (c, c), 1)
    tri = (row >= col).astype(F32)
    gcum_chunks, last_chunks = [], []
    for n in range(n_chunks):
        gc = jnp.dot(tri, g[n * c:(n + 1) * c], preferred_element_type=F32, precision=lax.Precision.HIGHEST)
        gcum_chunks.append(gc)
        last_chunks.append(jnp.broadcast_to(gc[c - 1:c, :], gc.shape))
    gcum = jnp.concatenate(gcum_chunks, axis=0)
    glast = jnp.concatenate(last_chunks, axis=0)
    out = jnp.where(grp == _BETA, beta,
                    jnp.where(grp == _GCUM, gcum,
                              jnp.where(grp == _EG, jnp.exp(gcum),
                                        jnp.where(grp == _EDEC, jnp.exp(glast - gcum), 0.0))))
    g_ref[...] = out
    for n in range(n_chunks):
        twice_t = jnp.concatenate([gcum_chunks[n], gcum_chunks[n]], axis=0).T
        for h in range(DN_HEADS):
            lane_idx = _GCUM * DN_HEADS + h
            gt_ref[h, n] = twice_t[lane_idx:lane_idx + 1, :]


def gates(xn, w_in_t, a_log, dt_bias, layer, *, tb=256):
    m, k = xn.shape
    tb = _tile(m, tb)
    depth = a_log.shape[0]
    tail_block = (w_in_t.shape[1] - 2 * DN_HEADS) // LANES
    assert tail_block * LANES + 2 * DN_HEADS == w_in_t.shape[1]

    def lane_row(p):
        z = jnp.zeros((depth, 1, DN_HEADS), F32)
        r = p.reshape(depth, 1, DN_HEADS)
        tail = jnp.zeros((depth, 1, LANES - (1 + _N_DECAY_COPIES) * DN_HEADS), F32)
        return jnp.concatenate([z] + [r] * _N_DECAY_COPIES + [tail], axis=-1)

    return pl.pallas_call(
        functools.partial(_gates_kernel, n_chunks=tb // DN_CHUNK),
        grid=(m // tb,),
        in_specs=[pl.BlockSpec((tb, k), lambda i: (i, 0)),
                  pl.BlockSpec((None, LANES, k), lambda i: (layer, tail_block, 0)),
                  pl.BlockSpec((None, 1, LANES), lambda i: (layer, 0, 0)),
                  pl.BlockSpec((None, 1, LANES), lambda i: (layer, 0, 0))],
        out_specs=[pl.BlockSpec((tb, LANES), lambda i: (i, 0)),
                   pl.BlockSpec((DN_HEADS, tb // DN_CHUNK, 1, 2 * DN_CHUNK), lambda i: (0, i, 0, 0))],
        out_shape=[jax.ShapeDtypeStruct((m, LANES), F32),
                   jax.ShapeDtypeStruct((DN_HEADS, m // DN_CHUNK, 1, 2 * DN_CHUNK), F32)],
        scratch_shapes=[pltpu.VMEM((k, LANES), BF16)],
        compiler_params=_params("arbitrary"),
        name="dn_gates",
    )(xn, w_in_t, lane_row(a_log), lane_row(dt_bias))


def _dot_nt(a, b):
    return lax.dot_general(a, b, (((1,), (1,)), ((), ())), preferred_element_type=F32)


def _bdot(a, b):
    return lax.dot_general(a, b, (((2,), (1,)), ((0,), (0,))), preferred_element_type=F32)


def _bdot_nt(a, b):
    return lax.dot_general(a, b, (((2,), (2,)), ((0,), (0,))), preferred_element_type=F32)


def _bdot_tn(a, b):
    return lax.dot_general(a, b, (((1,), (1,)), ((0,), (0,))), preferred_element_type=F32)


def _split(x):
    hi = x.astype(BF16)
    lo = (x - hi.astype(F32)).astype(BF16)
    return hi, lo


def _as_lhs(parts):
    hi, lo = parts
    return jnp.concatenate([hi, lo], axis=-1)


def _as_rhs(parts):
    hi, lo = parts
    return jnp.concatenate([hi, lo, hi, lo], axis=-2)


def _unit_lower_inverse2(n2):
    c = n2.shape[-2]
    row = lax.broadcasted_iota(jnp.int32, n2.shape[-2:], 0)
    col = lax.broadcasted_iota(jnp.int32, n2.shape[-2:], 1)
    eye2 = (row == (col % c)).astype(F32)
    inv = eye2 + n2
    parts = _split(n2)
    span = 2
    while span < c:
        n2 = _bdot(_as_lhs(parts), _as_rhs(parts))
        parts = _split(n2)
        inv = inv + _bdot(_as_lhs(_split(inv)), _as_rhs(parts))
        span *= 2
    return inv


def _deltanet_kernel(q_ref, k_ref, v_ref, z_ref, g_ref, gt_ref, cw_ref, nw_ref,
                     o_ref, state_ref, xbuf_ref, *, n_sub, sub_chunks):
    c = DN_CHUNK
    ts = sub_chunks * c
    tb = n_sub * ts
    hd = HEAD_DIM
    nh = DN_HEADS
    n_pair = nh * sub_chunks

    @pl.when(pl.program_id(1) == 0)
    def _():
        state_ref[...] = jnp.zeros_like(state_ref)
        xbuf_ref[:, 0:8, :] = jnp.zeros((3 * nh, 8, hd), F32)

    xbuf_ref[0:nh, 8:8 + tb, :] = q_ref[...].astype(F32)
    xbuf_ref[nh:2 * nh, 8:8 + tb, :] = k_ref[...].astype(F32)
    xbuf_ref[2 * nh:3 * nh, 8:8 + tb, :] = v_ref[...].astype(F32)

    def pairs(a):
        return a.reshape(n_pair, c, a.shape[-1])

    def unpair(a):
        return a.reshape(nh, sub_chunks, c, a.shape[-1])

    row2 = lax.broadcasted_iota(jnp.int32, (c, 2 * c), 0)
    col2 = lax.broadcasted_iota(jnp.int32, (c, 2 * c), 1) % c
    causal2 = row2 >= col2
    strict2 = row2 > col2

    def chunk_local(s):
        r0 = s * ts
        y = cw_ref[:, 0:1, :] * xbuf_ref[:, 5 + r0:5 + r0 + ts, :]
        for i in range(1, CONV_WIDTH):
            y = y + cw_ref[:, i:i + 1, :] * xbuf_ref[:, 5 + i + r0:5 + i + r0 + ts, :]
        y = _silu(y)
        q, k, v = y[0:nh], y[nh:2 * nh], y[2 * nh:3 * nh]
        qn = q * lax.rsqrt(jnp.sum(q * q, axis=-1, keepdims=True) + NORM_EPS) * (hd ** -0.5)
        kn = k * lax.rsqrt(jnp.sum(k * k, axis=-1, keepdims=True) + NORM_EPS)

        gates_blk = g_ref[r0:r0 + ts, :]
        lane = lax.broadcasted_iota(jnp.int32, gates_blk.shape, 1)

        def gate_cols(grp):
            return jnp.stack([jnp.sum(jnp.where(lane == grp * nh + h, gates_blk, 0.0), axis=-1, keepdims=True)
                              for h in range(nh)], axis=0)

        beta, gc, eg, edec = (gate_cols(grp) for grp in (_BETA, _GCUM, _EG, _EDEC))
        gr2 = gt_ref[:, s * sub_chunks:(s + 1) * sub_chunks].reshape(n_pair, 1, 2 * c)
        decay2 = jnp.where(causal2, jnp.exp(jnp.where(causal2, pairs(gc) - gr2, 0.0)), 0.0)
        kb = kn * beta
        kn_b = pairs(kn).astype(BF16)
        kkt2 = _bdot_nt(pairs(kb).astype(BF16), jnp.concatenate([kn_b, kn_b], axis=1)) * decay2
        inv2 = _unit_lower_inverse2(jnp.where(strict2, -kkt2, 0.0))
        rhs = pairs(jnp.concatenate([v * beta, kb * eg], axis=-1))
        sol = unpair(_bdot(_as_lhs(_split(inv2)), _as_rhs(_split(rhs))))
        qk = _bdot_nt(pairs(qn).astype(BF16), kn_b) * decay2[..., :c]
        return dict(u=sol[..., :hd], w=sol[..., hd:], qk=unpair(qk), qe=unpair(pairs(qn * eg)),
                    kd=unpair(pairs(kn * edec)), eg=unpair(pairs(eg)))

    local = [chunk_local(s) for s in range(n_sub)]
    xbuf_ref[:, 0:8, :] = xbuf_ref[:, tb:tb + 8, :]

    state = state_ref[...]
    for s in range(n_sub):
        part = local[s]
        for n in range(sub_chunks):
            r0 = s * ts + n * c
            state_b = state.astype(BF16)
            ws_qs = _bdot(jnp.concatenate([part["w"][:, n].astype(BF16), part["qe"][:, n].astype(BF16)], axis=1),
                          state_b)
            v_new = part["u"][:, n] - ws_qs[:, :c]
            v_new_b = v_new.astype(BF16)
            o_n = ws_qs[:, c:] + _bdot(part["qk"][:, n].astype(BF16), v_new_b)
            state = state * part["eg"][:, n, c - 1:c, :] + _bdot_tn(part["kd"][:, n].astype(BF16), v_new_b)
            out = _rms(o_n, nw_ref[...]) * _silu(z_ref[:, r0:r0 + c, :].astype(F32))
            for h in range(nh):
                o_ref[r0:r0 + c, h * hd:(h + 1) * hd] = out[h].astype(o_ref.dtype)
    state_ref[...] = state


def deltanet(proj_hm, gates_arr, gates_t, conv_w, dn_norm_w, layer, *, batch, tb=512, ts=256):
    _, m, _ = proj_hm.shape
    t_len = m // batch
    tb = _tile(t_len, tb)
    ts = _tile(tb, ts)
    n_t = t_len // tb
    hd = HEAD_DIM
    depth = conv_w.shape[0]
    cw = conv_w.reshape(depth, CONV_WIDTH, 3 * DN_HEADS, hd).transpose(0, 2, 1, 3)

    def grp(base):
        return pl.BlockSpec((DN_HEADS, tb, hd), lambda b, t: (base // DN_HEADS, b * n_t + t, 0))

    return pl.pallas_call(
        functools.partial(_deltanet_kernel, n_sub=tb // ts, sub_chunks=ts // DN_CHUNK),
        grid=(batch, n_t),
        in_specs=[grp(_Q_GRP), grp(_K_GRP), grp(_VV_GRP), grp(_Z_GRP),
                  pl.BlockSpec((tb, LANES), lambda b, t: (b * n_t + t, 0)),
                  pl.BlockSpec((DN_HEADS, tb // DN_CHUNK, 1, 2 * DN_CHUNK), lambda b, t: (0, b * n_t + t, 0, 0)),
                  pl.BlockSpec((None, 3 * DN_HEADS, CONV_WIDTH, hd), lambda b, t: (layer, 0, 0, 0)),
                  pl.BlockSpec((None, 1, hd), lambda b, t: (layer, 0, 0))],
        out_specs=pl.BlockSpec((tb, DN_HEADS * hd), lambda b, t: (b * n_t + t, 0)),
        out_shape=jax.ShapeDtypeStruct((m, DN_HEADS * hd), BF16),
        scratch_shapes=[pltpu.VMEM((DN_HEADS, hd, hd), F32),
                        pltpu.VMEM((3 * DN_HEADS, tb + 8, hd), F32)],
        compiler_params=_params("parallel", "arbitrary"),
        name="deltanet",
    )(proj_hm, proj_hm, proj_hm, proj_hm, gates_arr, gates_t, cw, dn_norm_w.reshape(depth, 1, hd))


def _xattn_kernel(q_ref, k_ref, v_ref, o_ref, *, head_dim):
    scale = head_dim ** -0.5
    for h in range(XA_HEADS):
        sl = slice(h * head_dim, (h + 1) * head_dim)
        s = _dot_nt(q_ref[:, sl], k_ref[:, sl]) * scale
        s = s - jnp.max(s, axis=-1, keepdims=True)
        e = jnp.exp(s)
        p = e / jnp.sum(e, axis=-1, keepdims=True)
        o = jnp.dot(p.astype(BF16), v_ref[:, sl], preferred_element_type=F32)
        o_ref[:, sl] = o.astype(o_ref.dtype)


def xattn(q, kv, *, batch, tm=512):
    m, d = q.shape
    t_len = m // batch
    n_mem = kv.shape[0] // batch
    tm = _tile(t_len, tm)
    n_t = t_len // tm
    return pl.pallas_call(
        functools.partial(_xattn_kernel, head_dim=d // XA_HEADS),
        grid=(batch, n_t),
        in_specs=[pl.BlockSpec((tm, d), lambda b, t: (b * n_t + t, 0)),
                  pl.BlockSpec((n_mem, d), lambda b, t: (b, 0)),
                  pl.BlockSpec((n_mem, d), lambda b, t: (b, 1))],
        out_specs=pl.BlockSpec((tm, d), lambda b, t: (b * n_t + t, 0)),
        out_shape=jax.ShapeDtypeStruct((m, d), BF16),
        compiler_params=_params("parallel", "parallel"),
        name="xattn",
    )(q, kv, kv)


def kernel(x, mem, ffn1_norm_pre, ffn1_w_gate_up, ffn1_w_down, ffn1_norm_post, mix_norm_pre, w_in, conv_w, a_log, dt_bias, sm_w, sm_b, sm_ln_g, sm_ln_b, dn_norm_w, w_out, mix_norm_post, xa_norm_pre, mem_norm, w_xq, w_xkv, w_xo, xa_norm_post, ffn2_norm_pre, ffn2_w_gate_up, ffn2_w_down, ffn2_norm_post):
    batch, t_len, d = x.shape
    depth = w_in.shape[0]
    d_ff = ffn1_w_down.shape[1]
    n_main = _N_GRP * LANES
    x2 = x.reshape(batch * t_len, d)
    mem2 = mem.reshape(-1, d)
    w_in_t = jnp.swapaxes(w_in, 1, 2)

    def ffn(xn, x2, w_gate_up, w_down, l, g_post, g_next):
        hmid, w_down_b = matmul_ws(xn, w_gate_up, l, mode="swiglu", n_cols=d_ff, col_offsets=(0, d_ff), tm=2048,
                                   side_cast=w_down)
        return matmul_post([hmid], w_down_b, l, x2, g_post, g_next, scale=0.5, tm=512, n_sub=4)

    xn = rmsnorm(x2, ffn1_norm_pre, 0)
    for l in range(depth):
        x2, xn = ffn(xn, x2, ffn1_w_gate_up, ffn1_w_down, l, (ffn1_norm_post, l), (mix_norm_pre, l))

        proj_hm = matmul_ws(xn, w_in_t, l, mode="headmajor", n_cols=n_main, tm=2048, tn=1024,
                            w_rows_are_outputs=True)
        y_a = sgu(proj_hm, sm_w, sm_b, sm_ln_g, sm_ln_b, l)
        gates_arr, gates_t = gates(xn, w_in_t, a_log, dt_bias, l)
        y_b = deltanet(proj_hm, gates_arr, gates_t, conv_w, dn_norm_w, l, batch=batch)
        x2, xn = matmul_post([y_a, y_b], w_out, l, x2, (mix_norm_post, l), (xa_norm_pre, l),
                             scale=1.0, tm=512, n_sub=4)

        q = matmul_ws(xn, w_xq, l, tm=2048, tn=1024)
        kv = matmul_ws(rmsnorm(mem2, mem_norm, l), w_xkv, l)
        att = xattn(q, kv, batch=batch)
        x2, xn = matmul_post([att], w_xo, l, x2, (xa_norm_post, l), (ffn2_norm_pre, l),
                             scale=1.0, tm=512, n_sub=4)

        g_next = (ffn1_norm_pre, l + 1) if l + 1 < depth else None
        x2, xn = ffn(xn, x2, ffn2_w_gate_up, ffn2_w_down, l, (ffn2_norm_post, l), g_next)
    return x2.reshape(batch, t_len, d)
```

```python
import functools

import jax
import jax.numpy as jnp
from jax import lax
from jax.experimental import pallas as pl
from jax.experimental.pallas import tpu as pltpu

F32 = jnp.float32
BF16 = jnp.bfloat16

NORM_EPS = 1e-6
HEAD_DIM = 128
GM_HEADS = 8
DN_HEADS = 8
GM_CHUNK = 128
DN_CHUNK = 64
CONV_WIDTH = 4
XA_HEADS = 4
LANES = 128
VMEM_LIMIT_BYTES = 58 * 1024 * 1024

_U_GRP, _V_GRP, _Q_GRP, _K_GRP, _VV_GRP, _Z_GRP = 0, 8, 16, 24, 32, 40
_N_GRP = 48
_BETA, _GCUM, _EG, _EDEC = 0, 1, 2, 3
_N_DECAY_COPIES = 3


def _params(*sem):
    return pltpu.CompilerParams(dimension_semantics=sem, vmem_limit_bytes=VMEM_LIMIT_BYTES)


def _tile(dim, want):
    t = min(dim, want)
    assert dim % t == 0, (dim, want)
    return t


def _rms(x, g):
    return x * lax.rsqrt(jnp.mean(x * x, axis=-1, keepdims=True) + NORM_EPS) * g


def _silu(x):
    return x * jax.nn.sigmoid(x)


def _gelu(x):
    return 0.5 * x * (1.0 + lax.erf(x * (2.0 ** -0.5)))


def _rmsnorm_kernel(x_ref, g_ref, o_ref):
    o_ref[...] = _rms(x_ref[...], g_ref[...]).astype(o_ref.dtype)


def rmsnorm(x, g_stack, layer, out_dtype=BF16, tm=512):
    m, d = x.shape
    tm = _tile(m, tm)
    return pl.pallas_call(
        _rmsnorm_kernel,
        grid=(m // tm,),
        in_specs=[pl.BlockSpec((tm, d), lambda i: (i, 0)),
                  pl.BlockSpec((None, 1, d), lambda i: (layer, 0, 0))],
        out_specs=pl.BlockSpec((tm, d), lambda i: (i, 0)),
        out_shape=jax.ShapeDtypeStruct((m, d), out_dtype),
        compiler_params=_params("parallel"),
        name="rmsnorm",
    )(x, g_stack.reshape(g_stack.shape[0], 1, d))


_ROWS_PER_DOT = 1024


def _mm_ws_kernel(*refs, n_w, mode, w_rows_are_outputs, has_side_cast):
    x_ref = refs[0]
    w_refs = refs[1:1 + n_w]
    if has_side_cast:
        side_ref, o_ref, side_out_ref, wb_ref = refs[1 + n_w:]
    else:
        o_ref, wb_ref = refs[1 + n_w:]
    tn = wb_ref.shape[1] // n_w

    @pl.when(pl.program_id(1) == 0)
    def _():
        for i, w_ref in enumerate(w_refs):
            w = w_ref[...]
            wb_ref[:, i * tn:(i + 1) * tn] = (w.T if w_rows_are_outputs else w).astype(BF16)
        if has_side_cast:
            side_out_ref[...] = side_ref[...].astype(BF16)

    tm = x_ref.shape[0]
    for r0 in range(0, tm, min(tm, _ROWS_PER_DOT)):
        rows = slice(r0, r0 + min(tm, _ROWS_PER_DOT))
        y = jnp.dot(x_ref[rows, :], wb_ref[...], preferred_element_type=F32)
        if mode == "swiglu":
            o_ref[rows, :] = (_silu(y[:, :tn]) * y[:, tn:]).astype(o_ref.dtype)
        elif mode == "headmajor":
            for c in range(tn // LANES):
                o_ref[c, rows, :] = y[:, c * LANES:(c + 1) * LANES].astype(o_ref.dtype)
        else:
            o_ref[rows, :] = y.astype(o_ref.dtype)


def matmul_ws(x, w_stack, layer, *, mode="plain", out_dtype=BF16, tm=1024, tn=512, n_cols=None, col_offsets=(0,),
              w_rows_are_outputs=False, side_cast=None):
    m, k = x.shape
    n = w_stack.shape[1 if w_rows_are_outputs else 2] if n_cols is None else n_cols
    tm, tn = _tile(m, tm), _tile(n, tn)
    n_w = len(col_offsets)
    assert all(off % tn == 0 for off in col_offsets)
    n_col_tiles = n // tn

    def w_spec(off):
        if w_rows_are_outputs:
            return pl.BlockSpec((None, tn, k), lambda j, i: (layer, j + off // tn, 0))
        return pl.BlockSpec((None, k, tn), lambda j, i: (layer, 0, j + off // tn))

    if mode == "headmajor":
        out_shape = jax.ShapeDtypeStruct((n // LANES, m, LANES), out_dtype)
        out_spec = pl.BlockSpec((tn // LANES, tm, LANES), lambda j, i: (j, i, 0))
    else:
        out_shape = jax.ShapeDtypeStruct((m, n), out_dtype)
        out_spec = pl.BlockSpec((tm, tn), lambda j, i: (i, j))
    in_specs = [pl.BlockSpec((tm, k), lambda j, i: (i, 0))] + [w_spec(off) for off in col_offsets]
    args = [x] + [w_stack] * n_w
    if side_cast is not None:
        _, side_rows, side_cols = side_cast.shape
        chunk = side_rows // n_col_tiles
        assert chunk * n_col_tiles == side_rows
        in_specs.append(pl.BlockSpec((None, chunk, side_cols), lambda j, i: (layer, j, 0)))
        args.append(side_cast)
        out_spec = [out_spec, pl.BlockSpec((chunk, side_cols), lambda j, i: (j, 0))]
        out_shape = [out_shape, jax.ShapeDtypeStruct((side_rows, side_cols), BF16)]
    return pl.pallas_call(
        functools.partial(_mm_ws_kernel, n_w=n_w, mode=mode, w_rows_are_outputs=w_rows_are_outputs,
                          has_side_cast=side_cast is not None),
        grid=(n_col_tiles, m // tm),
        in_specs=in_specs,
        out_specs=out_spec,
        out_shape=out_shape,
        scratch_shapes=[pltpu.VMEM((k, n_w * tn), BF16)],
        compiler_params=_params("arbitrary", "arbitrary"),
        name="mm_ws_" + mode,
    )(*args)


def _mm_post_kernel(*refs, n_act, n_cast, n_sub, scale, has_next):
    act_refs = refs[:n_act]
    w_ref, x_ref, gp_ref = refs[n_act:n_act + 3]
    rest = refs[n_act + 3:]
    gn_ref = xn_out_ref = None
    if has_next:
        gn_ref, x_out_ref, xn_out_ref = rest[:3]
        rest = rest[3:]
    else:
        x_out_ref = rest[0]
        rest = rest[1:]
    step = pl.program_id(0)
    k_each = act_refs[0].shape[1]

    if n_cast:
        (wb_ref,) = rest
        ck = w_ref.shape[0]

        @pl.when(step < n_cast)
        def _():
            wb_ref[pl.ds(pl.multiple_of(step * ck, ck), ck), :] = w_ref[...].astype(BF16)
    else:
        wb_ref = w_ref

    @pl.when(step >= n_cast)
    def _():
        ts = x_ref.shape[0] // n_sub
        for s in range(n_sub):
            rows = slice(s * ts, (s + 1) * ts)
            f = None
            for i, a_ref in enumerate(act_refs):
                d = jnp.dot(a_ref[rows, :], wb_ref[i * k_each:(i + 1) * k_each, :], preferred_element_type=F32)
                f = d if f is None else f + d
            x_new = x_ref[rows, :] + scale * _rms(f, gp_ref[...])
            x_out_ref[rows, :] = x_new
            if has_next:
                xn_out_ref[rows, :] = _rms(x_new, gn_ref[...]).astype(xn_out_ref.dtype)


def matmul_post(acts, w_stack, layer, x, g_post, g_next, *, scale, tm, n_sub, ck=512):
    m, d = x.shape
    tm = _tile(m, tm)
    n_act = len(acts)
    has_next = g_next is not None
    k_each = acts[0].shape[1]
    pre_cast = w_stack.ndim == 2
    k_all = w_stack.shape[-2]
    assert all(a.shape[1] == k_each for a in acts) and k_all == n_act * k_each
    if pre_cast:
        assert w_stack.dtype == BF16
        n_cast = 0
        w_spec = pl.BlockSpec((k_all, d), lambda i: (0, 0), pipeline_mode=pl.Buffered(1))
        scratch = []
    else:
        ck = _tile(k_all, ck)
        n_cast = k_all // ck
        w_spec = pl.BlockSpec((None, ck, d), lambda i: (layer, jnp.minimum(i, n_cast - 1), 0))
        scratch = [pltpu.VMEM((k_all, d), BF16)]

    def tile_idx(i):
        return jnp.maximum(i - n_cast, 0)

    act_specs = [pl.BlockSpec((tm, k_each), lambda i: (tile_idx(i), 0)) for _ in acts]
    row = pl.BlockSpec((tm, d), lambda i: (tile_idx(i), 0))

    def vec(pair):
        stack, lyr = pair
        return stack.reshape(stack.shape[0], 1, d), pl.BlockSpec((None, 1, d), lambda i: (lyr, 0, 0))

    gp_arr, gp_spec = vec(g_post)
    in_specs = act_specs + [w_spec, row, gp_spec]
    args = list(acts) + [w_stack, x, gp_arr]
    out_shape = [jax.ShapeDtypeStruct((m, d), F32)]
    out_specs = [row]
    if has_next:
        gn_arr, gn_spec = vec(g_next)
        in_specs.append(gn_spec)
        args.append(gn_arr)
        out_shape.append(jax.ShapeDtypeStruct((m, d), BF16))
        out_specs.append(row)
    outs = pl.pallas_call(
        functools.partial(_mm_post_kernel, n_act=n_act, n_cast=n_cast, n_sub=n_sub, scale=scale,
                          has_next=has_next),
        grid=(n_cast + m // tm,),
        in_specs=in_specs,
        out_specs=out_specs,
        out_shape=out_shape,
        scratch_shapes=scratch,
        compiler_params=_params("arbitrary"),
        name="mm_post",
    )(*args)
    return (outs[0], outs[1]) if has_next else (outs[0], None)


def _sgu_kernel(u_ref, v_ref, w_ref, bt_ref, lg_ref, lb_ref, o_ref, *, n_chunks):
    c = GM_CHUNK
    width = GM_HEADS * HEAD_DIM
    row = lax.broadcasted_iota(jnp.int32, (c, c), 0)
    col = lax.broadcasted_iota(jnp.int32, (c, c), 1)
    causal = row >= col
    vg = [_gelu(v_ref[h].astype(F32)) for h in range(GM_HEADS)]
    mu = sum(jnp.sum(x, axis=-1, keepdims=True) for x in vg) * (1.0 / width)
    var = sum(jnp.sum(jnp.square(x - mu), axis=-1, keepdims=True) for x in vg) * (1.0 / width)
    rstd = lax.rsqrt(var + NORM_EPS)
    for h in range(GM_HEADS):
        vn = ((vg[h] - mu) * rstd * lg_ref[h] + lb_ref[h]).astype(BF16)
        w = jnp.where(causal, w_ref[h], 0.0).astype(BF16)
        rhs = jnp.concatenate([vn[n * c:(n + 1) * c] for n in range(n_chunks)], axis=1)
        mixed = jnp.dot(w, rhs, preferred_element_type=F32) + bt_ref[:, h:h + 1]
        ug = _gelu(u_ref[h].astype(F32))
        for n in range(n_chunks):
            y = ug[n * c:(n + 1) * c] * mixed[:, n * HEAD_DIM:(n + 1) * HEAD_DIM]
            o_ref[n * c:(n + 1) * c, h * HEAD_DIM:(h + 1) * HEAD_DIM] = y.astype(o_ref.dtype)


def sgu(proj_hm, sm_w, sm_b, ln_g, ln_b, layer, *, tb=256):
    _, m, _ = proj_hm.shape
    tb = _tile(m, tb)
    width = GM_HEADS * HEAD_DIM
    depth = sm_w.shape[0]
    return pl.pallas_call(
        functools.partial(_sgu_kernel, n_chunks=tb // GM_CHUNK),
        grid=(m // tb,),
        in_specs=[pl.BlockSpec((GM_HEADS, tb, HEAD_DIM), lambda i: (_U_GRP // GM_HEADS, i, 0)),
                  pl.BlockSpec((GM_HEADS, tb, HEAD_DIM), lambda i: (_V_GRP // GM_HEADS, i, 0)),
                  pl.BlockSpec((None, GM_HEADS, GM_CHUNK, GM_CHUNK), lambda i: (layer, 0, 0, 0)),
                  pl.BlockSpec((None, GM_CHUNK, GM_HEADS), lambda i: (layer, 0, 0)),
                  pl.BlockSpec((None, GM_HEADS, 1, HEAD_DIM), lambda i: (layer, 0, 0, 0)),
                  pl.BlockSpec((None, GM_HEADS, 1, HEAD_DIM), lambda i: (layer, 0, 0, 0))],
        out_specs=pl.BlockSpec((tb, width), lambda i: (i, 0)),
        out_shape=jax.ShapeDtypeStruct((m, width), BF16),
        compiler_params=_params("parallel"),
        name="sgu",
    )(proj_hm, proj_hm, sm_w, sm_b.transpose(0, 2, 1),
      ln_g.reshape(depth, GM_HEADS, 1, HEAD_DIM), ln_b.reshape(depth, GM_HEADS, 1, HEAD_DIM))


def _gates_kernel(x_ref, w_ref, alog_ref, dtb_ref, g_ref, gt_ref, wb_ref, *, n_chunks):
    c = DN_CHUNK

    @pl.when(pl.program_id(0) == 0)
    def _():
        w = w_ref[...].T
        lane_w = lax.broadcasted_iota(jnp.int32, w.shape, 1)
        w_valid = jnp.where(lane_w < 2 * DN_HEADS, w, 0.0)
        w_decay = jnp.where(lane_w >= DN_HEADS, w_valid, 0.0)
        w_all = w_valid
        for r in range(1, _N_DECAY_COPIES):
            w_all = w_all + pltpu.roll(w_decay, r * DN_HEADS, axis=1)
        wb_ref[...] = w_all.astype(BF16)

    ba = jnp.dot(x_ref[...], wb_ref[...], preferred_element_type=F32)
    grp = lax.broadcasted_iota(jnp.int32, ba.shape, 1) // DN_HEADS
    beta = jax.nn.sigmoid(ba)
    g = -jnp.exp(alog_ref[...]) * jax.nn.softplus(ba + dtb_ref[...])
    row = lax.broadcasted_iota(jnp.int32, (c, c), 0)
    col = lax.broadcasted_iota(jnp.int32, (c, c), 1)
    tri = (row >= col).astype(F32)
    gcum_chunks, last_chunks = [], []
    for n in range(n_chunks):
        gc = jnp.dot(tri, g[n * c:(n + 1) * c], preferred_element_type=F32, precision=lax.Precision.HIGHEST)
        gcum_chunks.append(gc)
        last_chunks.append(jnp.broadcast_to(gc[c - 1:c, :], gc.shape))
    gcum = jnp.concatenate(gcum_chunks, axis=0)
    glast = jnp.concatenate(last_chunks, axis=0)
    out = jnp.where(grp == _BETA, beta,
                    jnp.where(grp == _GCUM, gcum,
                              jnp.where(grp == _EG, jnp.exp(gcum),
                                        jnp.where(grp == _EDEC, jnp.exp(glast - gcum), 0.0))))
    g_ref[...] = out
    for n in range(n_chunks):
        twice_t = jnp.concatenate([gcum_chunks[n], gcum_chunks[n]], axis=0).T
        for h in range(DN_HEADS):
            lane_idx = _GCUM * DN_HEADS + h
            gt_ref[h, n] = twice_t[lane_idx:lane_idx + 1, :]


def gates(xn, w_in_t, a_log, dt_bias, layer, *, tb=256):
    m, k = xn.shape
    tb = _tile(m, tb)
    depth = a_log.shape[0]
    tail_block = (w_in_t.shape[1] - 2 * DN_HEADS) // LANES
    assert tail_block * LANES + 2 * DN_HEADS == w_in_t.shape[1]

    def lane_row(p):
        z = jnp.zeros((depth, 1, DN_HEADS), F32)
        r = p.reshape(depth, 1, DN_HEADS)
        tail = jnp.zeros((depth, 1, LANES - (1 + _N_DECAY_COPIES) * DN_HEADS), F32)
        return jnp.concatenate([z] + [r] * _N_DECAY_COPIES + [tail], axis=-1)

    return pl.pallas_call(
        functools.partial(_gates_kernel, n_chunks=tb // DN_CHUNK),
        grid=(m // tb,),
        in_specs=[pl.BlockSpec((tb, k), lambda i: (i, 0)),
                  pl.BlockSpec((None, LANES, k), lambda i: (layer, tail_block, 0)),
                  pl.BlockSpec((None, 1, LANES), lambda i: (layer, 0, 0)),
                  pl.BlockSpec((None, 1, LANES), lambda i: (layer, 0, 0))],
        out_specs=[pl.BlockSpec((tb, LANES), lambda i: (i, 0)),
                   pl.BlockSpec((DN_HEADS, tb // DN_CHUNK, 1, 2 * DN_CHUNK), lambda i: (0, i, 0, 0))],
        out_shape=[jax.ShapeDtypeStruct((m, LANES), F32),
                   jax.ShapeDtypeStruct((DN_HEADS, m // DN_CHUNK, 1, 2 * DN_CHUNK), F32)],
        scratch_shapes=[pltpu.VMEM((k, LANES), BF16)],
        compiler_params=_params("arbitrary"),
        name="dn_gates",
    )(xn, w_in_t, lane_row(a_log), lane_row(dt_bias))


def _dot_nt(a, b):
    return lax.dot_general(a, b, (((1,), (1,)), ((), ())), preferred_element_type=F32)


def _bdot(a, b):
    return lax.dot_general(a, b, (((2,), (1,)), ((0,), (0,))), preferred_element_type=F32)


def _bdot_nt(a, b):
    return lax.dot_general(a, b, (((2,), (2,)), ((0,), (0,))), preferred_element_type=F32)


def _bdot_tn(a, b):
    return lax.dot_general(a, b, (((1,), (1,)), ((0,), (0,))), preferred_element_type=F32)


def _split(x):
    hi = x.astype(BF16)
    lo = (x - hi.astype(F32)).astype(BF16)
    return hi, lo


def _as_lhs(parts):
    hi, lo = parts
    return jnp.concatenate([hi, lo], axis=-1)


def _as_rhs(parts):
    hi, lo = parts
    return jnp.concatenate([hi, lo, hi, lo], axis=-2)


def _unit_lower_inverse2(n2):
    c = n2.shape[-2]
    row = lax.broadcasted_iota(jnp.int32, n2.shape[-2:], 0)
    col = lax.broadcasted_iota(jnp.int32, n2.shape[-2:], 1)
    eye2 = (row == (col % c)).astype(F32)
    inv = eye2 + n2
    parts = _split(n2)
    span = 2
    while span < c:
        n2 = _bdot(_as_lhs(parts), _as_rhs(parts))
        parts = _split(n2)
        inv = inv + _bdot(_as_lhs(_split(inv)), _as_rhs(parts))
        span *= 2
    return inv


def _deltanet_kernel(q_ref, k_ref, v_ref, z_ref, g_ref, gt_ref, cw_ref, nw_ref,
                     o_ref, state_ref, xbuf_ref, *, n_sub, sub_chunks):
    c = DN_CHUNK
    ts = sub_chunks * c
    tb = n_sub * ts
    hd = HEAD_DIM
    nh = DN_HEADS
    n_pair = nh * sub_chunks

    @pl.when(pl.program_id(1) == 0)
    def _():
        state_ref[...] = jnp.zeros_like(state_ref)
        xbuf_ref[:, 0:8, :] = jnp.zeros((3 * nh, 8, hd), F32)

    xbuf_ref[0:nh, 8:8 + tb, :] = q_ref[...].astype(F32)
    xbuf_ref[nh:2 * nh, 8:8 + tb, :] = k_ref[...].astype(F32)
    xbuf_ref[2 * nh:3 * nh, 8:8 + tb, :] = v_ref[...].astype(F32)

    def pairs(a):
        return a.reshape(n_pair, c, a.shape[-1])

    def unpair(a):
        return a.reshape(nh, sub_chunks, c, a.shape[-1])

    row2 = lax.broadcasted_iota(jnp.int32, (c, 2 * c), 0)
    col2 = lax.broadcasted_iota(jnp.int32, (c, 2 * c), 1) % c
    causal2 = row2 >= col2
    strict2 = row2 > col2

    def chunk_local(s):
        r0 = s * ts
        y = cw_ref[:, 0:1, :] * xbuf_ref[:, 5 + r0:5 + r0 + ts, :]
        for i in range(1, CONV_WIDTH):
            y = y + cw_ref[:, i:i + 1, :] * xbuf_ref[:, 5 + i + r0:5 + i + r0 + ts, :]
        y = _silu(y)
        q, k, v = y[0:nh], y[nh:2 * nh], y[2 * nh:3 * nh]
        qn = q * lax.rsqrt(jnp.sum(q * q, axis=-1, keepdims=True) + NORM_EPS) * (hd ** -0.5)
        kn = k * lax.rsqrt(jnp.sum(k * k, axis=-1, keepdims=True) + NORM_EPS)

        gates_blk = g_ref[r0:r0 + ts, :]
        lane = lax.broadcasted_iota(jnp.int32, gates_blk.shape, 1)

        def gate_cols(grp):
            return jnp.stack([jnp.sum(jnp.where(lane == grp * nh + h, gates_blk, 0.0), axis=-1, keepdims=True)
                              for h in range(nh)], axis=0)

        beta, gc, eg, edec = (gate_cols(grp) for grp in (_BETA, _GCUM, _EG, _EDEC))
        gr2 = gt_ref[:, s * sub_chunks:(s + 1) * sub_chunks].reshape(n_pair, 1, 2 * c)
        decay2 = jnp.where(causal2, jnp.exp(jnp.where(causal2, pairs(gc) - gr2, 0.0)), 0.0)
        kb = kn * beta
        kn_b = pairs(kn).astype(BF16)
        kkt2 = _bdot_nt(pairs(kb).astype(BF16), jnp.concatenate([kn_b, kn_b], axis=1)) * decay2
        inv2 = _unit_lower_inverse2(jnp.where(strict2, -kkt2, 0.0))
        rhs = pairs(jnp.concatenate([v * beta, kb * eg], axis=-1))
        sol = unpair(_bdot(_as_lhs(_split(inv2)), _as_rhs(_split(rhs))))
        qk = _bdot_nt(pairs(qn).astype(BF16), kn_b) * decay2[..., :c]
        return dict(u=sol[..., :hd], w=sol[..., hd:], qk=unpair(qk), qe=unpair(pairs(qn * eg)),
                    kd=unpair(pairs(kn * edec)), eg=unpair(pairs(eg)))

    local = [chunk_local(s) for s in range(n_sub)]
    xbuf_ref[:, 0:8, :] = xbuf_ref[:, tb:tb + 8, :]

    state = state_ref[...]
    for s in range(n_sub):
        part = local[s]
        for n in range(sub_chunks):
            r0 = s * ts + n * c
            state_b = state.astype(BF16)
            ws_qs = _bdot(jnp.concatenate([part["w"][:, n].astype(BF16), part["qe"][:, n].astype(BF16)], axis=1),
                          state_b)
            v_new = part["u"][:, n] - ws_qs[:, :c]
            v_new_b = v_new.astype(BF16)
            o_n = ws_qs[:, c:] + _bdot(part["qk"][:, n].astype(BF16), v_new_b)
            state = state * part["eg"][:, n, c - 1:c, :] + _bdot_tn(part["kd"][:, n].astype(BF16), v_new_b)
            out = _rms(o_n, nw_ref[...]) * _silu(z_ref[:, r0:r0 + c, :].astype(F32))
            for h in range(nh):
                o_ref[r0:r0 + c, h * hd:(h + 1) * hd] = out[h].astype(o_ref.dtype)
    state_ref[...] = state


def deltanet(proj_hm, gates_arr, gates_t, conv_w, dn_norm_w, layer, *, batch, tb=512, ts=256):
    _, m, _ = proj_hm.shape
    t_len = m // batch
    tb = _tile(t_len, tb)
    ts = _tile(tb, ts)
    n_t = t_len // tb
    hd = HEAD_DIM
    depth = conv_w.shape[0]
    cw = conv_w.reshape(depth, CONV_WIDTH, 3 * DN_HEADS, hd).transpose(0, 2, 1, 3)

    def grp(base):
        return pl.BlockSpec((DN_HEADS, tb, hd), lambda b, t: (base // DN_HEADS, b * n_t + t, 0))

    return pl.pallas_call(
        functools.partial(_deltanet_kernel, n_sub=tb // ts, sub_chunks=ts // DN_CHUNK),
        grid=(batch, n_t),
        in_specs=[grp(_Q_GRP), grp(_K_GRP), grp(_VV_GRP), grp(_Z_GRP),
                  pl.BlockSpec((tb, LANES), lambda b, t: (b * n_t + t, 0)),
                  pl.BlockSpec((DN_HEADS, tb // DN_CHUNK, 1, 2 * DN_CHUNK), lambda b, t: (0, b * n_t + t, 0, 0)),
                  pl.BlockSpec((None, 3 * DN_HEADS, CONV_WIDTH, hd), lambda b, t: (layer, 0, 0, 0)),
                  pl.BlockSpec((None, 1, hd), lambda b, t: (layer, 0, 0))],
        out_specs=pl.BlockSpec((tb, DN_HEADS * hd), lambda b, t: (b * n_t + t, 0)),
        out_shape=jax.ShapeDtypeStruct((m, DN_HEADS * hd), BF16),
        scratch_shapes=[pltpu.VMEM((DN_HEADS, hd, hd), F32),
                        pltpu.VMEM((3 * DN_HEADS, tb + 8, hd), F32)],
        compiler_params=_params("parallel", "arbitrary"),
        name="deltanet",
    )(proj_hm, proj_hm, proj_hm, proj_hm, gates_arr, gates_t, cw, dn_norm_w.reshape(depth, 1, hd))


def _xattn_fold_kernel(kv_ref, w_ref, o_ref, wb_ref, *, n_cast, fold_keys, head_dim, n_mem):
    step = pl.program_id(0)
    ck = w_ref.shape[0]

    @pl.when(step < n_cast)
    def _():
        wb_ref[pl.ds(pl.multiple_of(step * ck, ck), ck), :] = w_ref[...].astype(BF16)

    @pl.when(step >= n_cast)
    def _():
        for h in range(XA_HEADS):
            feat = slice(h * head_dim, (h + 1) * head_dim)
            toks = slice(h * n_mem, (h + 1) * n_mem)
            if fold_keys:
                o_ref[:, toks] = _dot_nt(wb_ref[:, feat], kv_ref[:, feat]).astype(o_ref.dtype)
            else:
                o_ref[toks, :] = jnp.dot(kv_ref[:, feat], wb_ref[feat, :],
                                         preferred_element_type=F32).astype(o_ref.dtype)


def xattn_fold(kv, w_stack, layer, *, batch, fold_keys, ck=512):
    n_mem = kv.shape[0] // batch
    d = w_stack.shape[1]
    ck = _tile(d, ck)
    n_cast = d // ck
    out_dims = (d, XA_HEADS * n_mem) if fold_keys else (XA_HEADS * n_mem, d)
    return pl.pallas_call(
        functools.partial(_xattn_fold_kernel, n_cast=n_cast, fold_keys=fold_keys, head_dim=d // XA_HEADS,
                          n_mem=n_mem),
        grid=(n_cast + batch,),
        in_specs=[pl.BlockSpec((n_mem, d), lambda i: (jnp.maximum(i - n_cast, 0), 0 if fold_keys else 1)),
                  pl.BlockSpec((None, ck, d), lambda i: (layer, jnp.minimum(i, n_cast - 1), 0))],
        out_specs=pl.BlockSpec((None,) + out_dims, lambda i: (jnp.maximum(i - n_cast, 0), 0, 0)),
        out_shape=jax.ShapeDtypeStruct((batch,) + out_dims, BF16),
        scratch_shapes=[pltpu.VMEM((d, d), BF16)],
        compiler_params=_params("arbitrary"),
        name="xattn_fold",
    )(kv, w_stack)


def _xattn_post_kernel(xn_ref, wk_ref, vw_ref, x_ref, gp_ref, gn_ref, x_out_ref, xn_out_ref, *,
                       n_sub, head_dim, n_mem):
    ts = x_ref.shape[0] // n_sub
    scale = head_dim ** -0.5
    for s in range(n_sub):
        rows = slice(s * ts, (s + 1) * ts)
        logits = jnp.dot(xn_ref[rows, :], wk_ref[...], preferred_element_type=F32) * scale
        probs = []
        for h in range(XA_HEADS):
            l_h = logits[:, h * n_mem:(h + 1) * n_mem]
            e = jnp.exp(l_h - jnp.max(l_h, axis=-1, keepdims=True))
            probs.append((e / jnp.sum(e, axis=-1, keepdims=True)).astype(BF16))
        c = jnp.dot(jnp.concatenate(probs, axis=-1), vw_ref[...], preferred_element_type=F32)
        x_new = x_ref[rows, :] + _rms(c, gp_ref[...])
        x_out_ref[rows, :] = x_new
        xn_out_ref[rows, :] = _rms(x_new, gn_ref[...]).astype(xn_out_ref.dtype)


def xattn_post(xn, wk, vw, x, g_post, g_next, *, batch, tm=512, n_sub=2):
    m, d = x.shape
    t_len = m // batch
    tm = _tile(t_len, tm)
    tiles_per_seq = t_len // tm
    n_mem = wk.shape[2] // XA_HEADS
    row = pl.BlockSpec((tm, d), lambda i: (i, 0))

    def vec(pair):
        stack, lyr = pair
        return stack.reshape(stack.shape[0], 1, d), pl.BlockSpec((None, 1, d), lambda i: (lyr, 0, 0))

    gp_arr, gp_spec = vec(g_post)
    gn_arr, gn_spec = vec(g_next)
    return pl.pallas_call(
        functools.partial(_xattn_post_kernel, n_sub=n_sub, head_dim=d // XA_HEADS, n_mem=n_mem),
        grid=(m // tm,),
        in_specs=[row,
                  pl.BlockSpec((None,) + wk.shape[1:], lambda i: (i // tiles_per_seq, 0, 0)),
                  pl.BlockSpec((None,) + vw.shape[1:], lambda i: (i // tiles_per_seq, 0, 0)),
                  row, gp_spec, gn_spec],
        out_specs=[row, row],
        out_shape=[jax.ShapeDtypeStruct((m, d), F32), jax.ShapeDtypeStruct((m, d), BF16)],
        compiler_params=_params("parallel"),
        name="xattn_post",
    )(xn, wk, vw, x, gp_arr, gn_arr)


def kernel(x, mem, ffn1_norm_pre, ffn1_w_gate_up, ffn1_w_down, ffn1_norm_post, mix_norm_pre, w_in, conv_w, a_log, dt_bias, sm_w, sm_b, sm_ln_g, sm_ln_b, dn_norm_w, w_out, mix_norm_post, xa_norm_pre, mem_norm, w_xq, w_xkv, w_xo, xa_norm_post, ffn2_norm_pre, ffn2_w_gate_up, ffn2_w_down, ffn2_norm_post):
    batch, t_len, d = x.shape
    depth = w_in.shape[0]
    d_ff = ffn1_w_down.shape[1]
    n_main = _N_GRP * LANES
    x2 = x.reshape(batch * t_len, d)
    mem2 = mem.reshape(-1, d)
    w_in_t = jnp.swapaxes(w_in, 1, 2)

    def ffn(xn, x2, w_gate_up, w_down, l, g_post, g_next):
        hmid, w_down_b = matmul_ws(xn, w_gate_up, l, mode="swiglu", n_cols=d_ff, col_offsets=(0, d_ff), tm=2048,
                                   side_cast=w_down)
        return matmul_post([hmid], w_down_b, l, x2, g_post, g_next, scale=0.5, tm=512, n_sub=4)

    xn = rmsnorm(x2, ffn1_norm_pre, 0)
    for l in range(depth):
        x2, xn = ffn(xn, x2, ffn1_w_gate_up, ffn1_w_down, l, (ffn1_norm_post, l), (mix_norm_pre, l))

        proj_hm = matmul_ws(xn, w_in_t, l, mode="headmajor", n_cols=n_main, tm=2048, tn=1024,
                            w_rows_are_outputs=True)
        y_a = sgu(proj_hm, sm_w, sm_b, sm_ln_g, sm_ln_b, l)
        gates_arr, gates_t = gates(xn, w_in_t, a_log, dt_bias, l)
        y_b = deltanet(proj_hm, gates_arr, gates_t, conv_w, dn_norm_w, l, batch=batch)
        x2, xn = matmul_post([y_a, y_b], w_out, l, x2, (mix_norm_post, l), (xa_norm_pre, l),
                             scale=1.0, tm=512, n_sub=4)

        kv = matmul_ws(rmsnorm(mem2, mem_norm, l), w_xkv, l)
        wk = xattn_fold(kv, w_xq, l, batch=batch, fold_keys=True)
        vw = xattn_fold(kv, w_xo, l, batch=batch, fold_keys=False)
        x2, xn = xattn_post(xn, wk, vw, x2, (xa_norm_post, l), (ffn2_norm_pre, l), batch=batch)

        g_next = (ffn1_norm_pre, l + 1) if l + 1 < depth else None
        x2, xn = ffn(xn, x2, ffn2_w_gate_up, ffn2_w_down, l, (ffn2_norm_post, l), g_next)
    return x2.reshape(batch, t_len, d)
```

```python
import functools

import jax
import jax.numpy as jnp
from jax import lax
from jax.experimental import pallas as pl
from jax.experimental.pallas import tpu as pltpu

F32 = jnp.float32
BF16 = jnp.bfloat16

NORM_EPS = 1e-6
HEAD_DIM = 128
GM_HEADS = 8
DN_HEADS = 8
GM_CHUNK = 128
DN_CHUNK = 64
CONV_WIDTH = 4
XA_HEADS = 4
LANES = 128
VMEM_LIMIT_BYTES = 58 * 1024 * 1024

_U_GRP, _V_GRP, _Q_GRP, _K_GRP, _VV_GRP, _Z_GRP = 0, 8, 16, 24, 32, 40
_N_GRP = 48
_BETA, _GCUM, _EG, _EDEC = 0, 1, 2, 3
_N_DECAY_COPIES = 3


def _params(*sem):
    return pltpu.CompilerParams(dimension_semantics=sem, vmem_limit_bytes=VMEM_LIMIT_BYTES)


def _tile(dim, want):
    t = min(dim, want)
    assert dim % t == 0, (dim, want)
    return t


def _rms(x, g):
    return x * lax.rsqrt(jnp.mean(x * x, axis=-1, keepdims=True) + NORM_EPS) * g


def _silu(x):
    return x * jax.nn.sigmoid(x)


def _gelu(x):
    return 0.5 * x * (1.0 + lax.erf(x * (2.0 ** -0.5)))


def _rmsnorm_kernel(x_ref, g_ref, o_ref):
    o_ref[...] = _rms(x_ref[...], g_ref[...]).astype(o_ref.dtype)


def rmsnorm(x, g_stack, layer, out_dtype=BF16, tm=512):
    m, d = x.shape
    tm = _tile(m, tm)
    return pl.pallas_call(
        _rmsnorm_kernel,
        grid=(m // tm,),
        in_specs=[pl.BlockSpec((tm, d), lambda i: (i, 0)),
                  pl.BlockSpec((None, 1, d), lambda i: (layer, 0, 0))],
        out_specs=pl.BlockSpec((tm, d), lambda i: (i, 0)),
        out_shape=jax.ShapeDtypeStruct((m, d), out_dtype),
        compiler_params=_params("parallel"),
        name="rmsnorm",
    )(x, g_stack.reshape(g_stack.shape[0], 1, d))


_ROWS_PER_DOT = 1024


def _mm_ws_kernel(*refs, n_w, mode, w_rows_are_outputs, has_side_cast):
    x_ref = refs[0]
    w_refs = refs[1:1 + n_w]
    if has_side_cast:
        side_ref, o_ref, side_out_ref, wb_ref = refs[1 + n_w:]
    else:
        o_ref, wb_ref = refs[1 + n_w:]
    tn = wb_ref.shape[1] // n_w

    @pl.when(pl.program_id(1) == 0)
    def _():
        for i, w_ref in enumerate(w_refs):
            w = w_ref[...]
            wb_ref[:, i * tn:(i + 1) * tn] = (w.T if w_rows_are_outputs else w).astype(BF16)
        if has_side_cast:
            side_out_ref[...] = side_ref[...].astype(BF16)

    tm = x_ref.shape[0]
    for r0 in range(0, tm, min(tm, _ROWS_PER_DOT)):
        rows = slice(r0, r0 + min(tm, _ROWS_PER_DOT))
        y = jnp.dot(x_ref[rows, :], wb_ref[...], preferred_element_type=F32)
        if mode == "swiglu":
            o_ref[rows, :] = (_silu(y[:, :tn]) * y[:, tn:]).astype(o_ref.dtype)
        elif mode == "headmajor":
            for c in range(tn // LANES):
                o_ref[c, rows, :] = y[:, c * LANES:(c + 1) * LANES].astype(o_ref.dtype)
        else:
            o_ref[rows, :] = y.astype(o_ref.dtype)


def matmul_ws(x, w_stack, layer, *, mode="plain", out_dtype=BF16, tm=1024, tn=512, n_cols=None, col_offsets=(0,),
              w_rows_are_outputs=False, side_cast=None):
    m, k = x.shape
    n = w_stack.shape[1 if w_rows_are_outputs else 2] if n_cols is None else n_cols
    tm, tn = _tile(m, tm), _tile(n, tn)
    n_w = len(col_offsets)
    assert all(off % tn == 0 for off in col_offsets)
    n_col_tiles = n // tn

    def w_spec(off):
        if w_rows_are_outputs:
            return pl.BlockSpec((None, tn, k), lambda j, i: (layer, j + off // tn, 0))
        return pl.BlockSpec((None, k, tn), lambda j, i: (layer, 0, j + off // tn))

    if mode == "headmajor":
        out_shape = jax.ShapeDtypeStruct((n // LANES, m, LANES), out_dtype)
        out_spec = pl.BlockSpec((tn // LANES, tm, LANES), lambda j, i: (j, i, 0))
    else:
        out_shape = jax.ShapeDtypeStruct((m, n), out_dtype)
        out_spec = pl.BlockSpec((tm, tn), lambda j, i: (i, j))
    in_specs = [pl.BlockSpec((tm, k), lambda j, i: (i, 0))] + [w_spec(off) for off in col_offsets]
    args = [x] + [w_stack] * n_w
    if side_cast is not None:
        _, side_rows, side_cols = side_cast.shape
        chunk = side_rows // n_col_tiles
        assert chunk * n_col_tiles == side_rows
        in_specs.append(pl.BlockSpec((None, chunk, side_cols), lambda j, i: (layer, j, 0)))
        args.append(side_cast)
        out_spec = [out_spec, pl.BlockSpec((chunk, side_cols), lambda j, i: (j, 0))]
        out_shape = [out_shape, jax.ShapeDtypeStruct((side_rows, side_cols), BF16)]
    return pl.pallas_call(
        functools.partial(_mm_ws_kernel, n_w=n_w, mode=mode, w_rows_are_outputs=w_rows_are_outputs,
                          has_side_cast=side_cast is not None),
        grid=(n_col_tiles, m // tm),
        in_specs=in_specs,
        out_specs=out_spec,
        out_shape=out_shape,
        scratch_shapes=[pltpu.VMEM((k, n_w * tn), BF16)],
        compiler_params=_params("arbitrary", "arbitrary"),
        name="mm_ws_" + mode,
    )(*args)


def _mm_post_kernel(*refs, n_act, n_cast, n_sub, scale, has_next):
    act_refs = refs[:n_act]
    w_ref, x_ref, gp_ref = refs[n_act:n_act + 3]
    rest = refs[n_act + 3:]
    gn_ref = xn_out_ref = None
    if has_next:
        gn_ref, x_out_ref, xn_out_ref = rest[:3]
        rest = rest[3:]
    else:
        x_out_ref = rest[0]
        rest = rest[1:]
    step = pl.program_id(0)
    k_each = act_refs[0].shape[1]

    if n_cast:
        (wb_ref,) = rest
        ck = w_ref.shape[0]

        @pl.when(step < n_cast)
        def _():
            wb_ref[pl.ds(pl.multiple_of(step * ck, ck), ck), :] = w_ref[...].astype(BF16)
    else:
        wb_ref = w_ref

    @pl.when(step >= n_cast)
    def _():
        ts = x_ref.shape[0] // n_sub
        for s in range(n_sub):
            rows = slice(s * ts, (s + 1) * ts)
            f = None
            for i, a_ref in enumerate(act_refs):
                d = jnp.dot(a_ref[rows, :], wb_ref[i * k_each:(i + 1) * k_each, :], preferred_element_type=F32)
                f = d if f is None else f + d
            x_new = x_ref[rows, :] + scale * _rms(f, gp_ref[...])
            x_out_ref[rows, :] = x_new
            if has_next:
                xn_out_ref[rows, :] = _rms(x_new, gn_ref[...]).astype(xn_out_ref.dtype)


def matmul_post(acts, w_stack, layer, x, g_post, g_next, *, scale, tm, n_sub, ck=512):
    m, d = x.shape
    tm = _tile(m, tm)
    n_act = len(acts)
    has_next = g_next is not None
    k_each = acts[0].shape[1]
    pre_cast = w_stack.ndim == 2
    k_all = w_stack.shape[-2]
    assert all(a.shape[1] == k_each for a in acts) and k_all == n_act * k_each
    if pre_cast:
        assert w_stack.dtype == BF16
        n_cast = 0
        w_spec = pl.BlockSpec((k_all, d), lambda i: (0, 0), pipeline_mode=pl.Buffered(1))
        scratch = []
    else:
        ck = _tile(k_all, ck)
        n_cast = k_all // ck
        w_spec = pl.BlockSpec((None, ck, d), lambda i: (layer, jnp.minimum(i, n_cast - 1), 0))
        scratch = [pltpu.VMEM((k_all, d), BF16)]

    def tile_idx(i):
        return jnp.maximum(i - n_cast, 0)

    act_specs = [pl.BlockSpec((tm, k_each), lambda i: (tile_idx(i), 0)) for _ in acts]
    row = pl.BlockSpec((tm, d), lambda i: (tile_idx(i), 0))

    def vec(pair):
        stack, lyr = pair
        return stack.reshape(stack.shape[0], 1, d), pl.BlockSpec((None, 1, d), lambda i: (lyr, 0, 0))

    gp_arr, gp_spec = vec(g_post)
    in_specs = act_specs + [w_spec, row, gp_spec]
    args = list(acts) + [w_stack, x, gp_arr]
    out_shape = [jax.ShapeDtypeStruct((m, d), F32)]
    out_specs = [row]
    if has_next:
        gn_arr, gn_spec = vec(g_next)
        in_specs.append(gn_spec)
        args.append(gn_arr)
        out_shape.append(jax.ShapeDtypeStruct((m, d), BF16))
        out_specs.append(row)
    outs = pl.pallas_call(
        functools.partial(_mm_post_kernel, n_act=n_act, n_cast=n_cast, n_sub=n_sub, scale=scale,
                          has_next=has_next),
        grid=(n_cast + m // tm,),
        in_specs=in_specs,
        out_specs=out_specs,
        out_shape=out_shape,
        scratch_shapes=scratch,
        compiler_params=_params("arbitrary"),
        name="mm_post",
    )(*args)
    return (outs[0], outs[1]) if has_next else (outs[0], None)


def _sgu_rows(u_ref, v_ref, w_ref, bt_ref, lg_ref, lb_ref, r0, n_chunks):
    c = GM_CHUNK
    rows = slice(r0, r0 + n_chunks * c)
    width = GM_HEADS * HEAD_DIM
    row = lax.broadcasted_iota(jnp.int32, (c, c), 0)
    col = lax.broadcasted_iota(jnp.int32, (c, c), 1)
    causal = row >= col
    vg = [_gelu(v_ref[h, rows, :].astype(F32)) for h in range(GM_HEADS)]
    mu = jnp.sum(sum(vg), axis=-1, keepdims=True) * (1.0 / width)
    var = jnp.sum(sum(jnp.square(x - mu) for x in vg), axis=-1, keepdims=True) * (1.0 / width)
    rstd = lax.rsqrt(var + NORM_EPS)
    heads = []
    for h in range(GM_HEADS):
        vn = ((vg[h] - mu) * rstd * lg_ref[h] + lb_ref[h]).astype(BF16)
        w = jnp.where(causal, w_ref[h], 0.0).astype(BF16)
        rhs = jnp.concatenate([vn[n * c:(n + 1) * c] for n in range(n_chunks)], axis=1)
        mixed = jnp.dot(w, rhs, preferred_element_type=F32) + bt_ref[:, h:h + 1]
        mixed = jnp.concatenate([mixed[:, n * HEAD_DIM:(n + 1) * HEAD_DIM] for n in range(n_chunks)], axis=0)
        heads.append((_gelu(u_ref[h, rows, :].astype(F32)) * mixed).astype(BF16))
    return jnp.concatenate(heads, axis=-1)


def _mix_out_kernel(u_ref, v_ref, sw_ref, bt_ref, lg_ref, lb_ref, yb_ref, w_ref, x_ref, gp_ref, gn_ref,
                    x_out_ref, xn_out_ref, wb_ref, *, n_cast, n_sub):
    step = pl.program_id(0)
    ck = w_ref.shape[0]
    ka = GM_HEADS * HEAD_DIM

    @pl.when(step < n_cast)
    def _():
        wb_ref[pl.ds(pl.multiple_of(step * ck, ck), ck), :] = w_ref[...].astype(BF16)

    @pl.when(step >= n_cast)
    def _():
        ts = x_ref.shape[0] // n_sub
        for s in range(n_sub):
            rows = slice(s * ts, (s + 1) * ts)
            y_a = _sgu_rows(u_ref, v_ref, sw_ref, bt_ref, lg_ref, lb_ref, s * ts, ts // GM_CHUNK)
            f = (jnp.dot(y_a, wb_ref[0:ka, :], preferred_element_type=F32)
                 + jnp.dot(yb_ref[rows, :], wb_ref[ka:, :], preferred_element_type=F32))
            x_new = x_ref[rows, :] + _rms(f, gp_ref[...])
            x_out_ref[rows, :] = x_new
            xn_out_ref[rows, :] = _rms(x_new, gn_ref[...]).astype(xn_out_ref.dtype)


def mix_out(proj_hm, y_b, sm_w, sm_b, ln_g, ln_b, w_out, layer, x, g_post, g_next, *, tm=512, n_sub=4, ck=512):
    m, d = x.shape
    tm = _tile(m, tm)
    depth = sm_w.shape[0]
    k_all = w_out.shape[1]
    ck = _tile(k_all, ck)
    n_cast = k_all // ck

    def tile_idx(i):
        return jnp.maximum(i - n_cast, 0)

    row = pl.BlockSpec((tm, d), lambda i: (tile_idx(i), 0))

    def vec(pair):
        stack, lyr = pair
        return stack.reshape(stack.shape[0], 1, d), pl.BlockSpec((None, 1, d), lambda i: (lyr, 0, 0))

    gp_arr, gp_spec = vec(g_post)
    gn_arr, gn_spec = vec(g_next)
    outs = pl.pallas_call(
        functools.partial(_mix_out_kernel, n_cast=n_cast, n_sub=n_sub),
        grid=(n_cast + m // tm,),
        in_specs=[pl.BlockSpec((GM_HEADS, tm, HEAD_DIM), lambda i: (_U_GRP // GM_HEADS, tile_idx(i), 0)),
                  pl.BlockSpec((GM_HEADS, tm, HEAD_DIM), lambda i: (_V_GRP // GM_HEADS, tile_idx(i), 0)),
                  pl.BlockSpec((None, GM_HEADS, GM_CHUNK, GM_CHUNK), lambda i: (layer, 0, 0, 0)),
                  pl.BlockSpec((None, GM_CHUNK, GM_HEADS), lambda i: (layer, 0, 0)),
                  pl.BlockSpec((None, GM_HEADS, 1, HEAD_DIM), lambda i: (layer, 0, 0, 0)),
                  pl.BlockSpec((None, GM_HEADS, 1, HEAD_DIM), lambda i: (layer, 0, 0, 0)),
                  pl.BlockSpec((tm, y_b.shape[1]), lambda i: (tile_idx(i), 0)),
                  pl.BlockSpec((None, ck, d), lambda i: (layer, jnp.minimum(i, n_cast - 1), 0)),
                  row, gp_spec, gn_spec],
        out_specs=[row, row],
        out_shape=[jax.ShapeDtypeStruct((m, d), F32), jax.ShapeDtypeStruct((m, d), BF16)],
        scratch_shapes=[pltpu.VMEM((k_all, d), BF16)],
        compiler_params=_params("arbitrary"),
        name="mix_out",
    )(proj_hm, proj_hm, sm_w, sm_b.transpose(0, 2, 1),
      ln_g.reshape(depth, GM_HEADS, 1, HEAD_DIM), ln_b.reshape(depth, GM_HEADS, 1, HEAD_DIM),
      y_b, w_out, x, gp_arr, gn_arr)
    return outs[0], outs[1]


def _gates_kernel(x_ref, w_ref, alog_ref, dtb_ref, g_ref, gt_ref, wb_ref, *, n_chunks):
    c = DN_CHUNK

    @pl.when(pl.program_id(0) == 0)
    def _():
        w = w_ref[...].T
        lane_w = lax.broadcasted_iota(jnp.int32, w.shape, 1)
        w_valid = jnp.where(lane_w < 2 * DN_HEADS, w, 0.0)
        w_decay = jnp.where(lane_w >= DN_HEADS, w_valid, 0.0)
        w_all = w_valid
        for r in range(1, _N_DECAY_COPIES):
            w_all = w_all + pltpu.roll(w_decay, r * DN_HEADS, axis=1)
        wb_ref[...] = w_all.astype(BF16)

    ba = jnp.dot(x_ref[...], wb_ref[...], preferred_element_type=F32)
    grp = lax.broadcasted_iota(jnp.int32, ba.shape, 1) // DN_HEADS
    beta = jax.nn.sigmoid(ba)
    g = -jnp.exp(alog_ref[...]) * jax.nn.softplus(ba + dtb_ref[...])
    row = lax.broadcasted_iota(jnp.int32, (c, c), 0)
    col = lax.broadcasted_iota(jnp.int32, (c, c), 1)
    tri = (row >= col).astype(F32)
    gcum_chunks, last_chunks = [], []
    for n in range(n_chunks):
        gc = jnp.dot(tri, g[n * c:(n + 1) * c], preferred_element_type=F32, precision=lax.Precision.HIGHEST)
        gcum_chunks.append(gc)
        last_chunks.append(jnp.broadcast_to(gc[c - 1:c, :], gc.shape))
    gcum = jnp.concatenate(gcum_chunks, axis=0)
    glast = jnp.concatenate(last_chunks, axis=0)
    out = jnp.where(grp == _BETA, beta,
                    jnp.where(grp == _GCUM, gcum,
                              jnp.where(grp == _EG, jnp.exp(gcum),
                                        jnp.where(grp == _EDEC, jnp.exp(glast - gcum), 0.0))))
    g_ref[...] = out
    for n in range(n_chunks):
        twice_t = jnp.concatenate([gcum_chunks[n], gcum_chunks[n]], axis=0).T
        for h in range(DN_HEADS):
            lane_idx = _GCUM * DN_HEADS + h
            gt_ref[h, n] = twice_t[lane_idx:lane_idx + 1, :]


def gates(xn, w_in_t, a_log, dt_bias, layer, *, tb=256):
    m, k = xn.shape
    tb = _tile(m, tb)
    depth = a_log.shape[0]
    tail_block = (w_in_t.shape[1] - 2 * DN_HEADS) // LANES
    assert tail_block * LANES + 2 * DN_HEADS == w_in_t.shape[1]

    def lane_row(p):
        z = jnp.zeros((depth, 1, DN_HEADS), F32)
        r = p.reshape(depth, 1, DN_HEADS)
        tail = jnp.zeros((depth, 1, LANES - (1 + _N_DECAY_COPIES) * DN_HEADS), F32)
        return jnp.concatenate([z] + [r] * _N_DECAY_COPIES + [tail], axis=-1)

    return pl.pallas_call(
        functools.partial(_gates_kernel, n_chunks=tb // DN_CHUNK),
        grid=(m // tb,),
        in_specs=[pl.BlockSpec((tb, k), lambda i: (i, 0)),
                  pl.BlockSpec((None, LANES, k), lambda i: (layer, tail_block, 0)),
                  pl.BlockSpec((None, 1, LANES), lambda i: (layer, 0, 0)),
                  pl.BlockSpec((None, 1, LANES), lambda i: (layer, 0, 0))],
        out_specs=[pl.BlockSpec((tb, LANES), lambda i: (i, 0)),
                   pl.BlockSpec((DN_HEADS, tb // DN_CHUNK, 1, 2 * DN_CHUNK), lambda i: (0, i, 0, 0))],
        out_shape=[jax.ShapeDtypeStruct((m, LANES), F32),
                   jax.ShapeDtypeStruct((DN_HEADS, m // DN_CHUNK, 1, 2 * DN_CHUNK), F32)],
        scratch_shapes=[pltpu.VMEM((k, LANES), BF16)],
        compiler_params=_params("arbitrary"),
        name="dn_gates",
    )(xn, w_in_t, lane_row(a_log), lane_row(dt_bias))


def _dot_nt(a, b):
    return lax.dot_general(a, b, (((1,), (1,)), ((), ())), preferred_element_type=F32)


def _bdot(a, b):
    return lax.dot_general(a, b, (((2,), (1,)), ((0,), (0,))), preferred_element_type=F32)


def _bdot_nt(a, b):
    return lax.dot_general(a, b, (((2,), (2,)), ((0,), (0,))), preferred_element_type=F32)


def _bdot_tn(a, b):
    return lax.dot_general(a, b, (((1,), (1,)), ((0,), (0,))), preferred_element_type=F32)


def _split(x):
    hi = x.astype(BF16)
    lo = (x - hi.astype(F32)).astype(BF16)
    return hi, lo


def _as_lhs(parts):
    hi, lo = parts
    return jnp.concatenate([hi, lo], axis=-1)


def _as_rhs(parts):
    hi, lo = parts
    return jnp.concatenate([hi, lo, hi, lo], axis=-2)


def _unit_lower_inverse2(n2):
    c = n2.shape[-2]
    row = lax.broadcasted_iota(jnp.int32, n2.shape[-2:], 0)
    col = lax.broadcasted_iota(jnp.int32, n2.shape[-2:], 1)
    eye2 = (row == (col % c)).astype(F32)
    inv = eye2 + n2
    parts = _split(n2)
    span = 2
    while span < c:
        n2 = _bdot(_as_lhs(parts), _as_rhs(parts))
        parts = _split(n2)
        inv = inv + _bdot(_as_lhs(_split(inv)), _as_rhs(parts))
        span *= 2
    return inv


def _deltanet_kernel(q_ref, k_ref, v_ref, z_ref, g_ref, gt_ref, cw_ref, nw_ref,
                     o_ref, state_ref, xbuf_ref, *, n_sub, sub_chunks):
    c = DN_CHUNK
    ts = sub_chunks * c
    tb = n_sub * ts
    hd = HEAD_DIM
    nh = DN_HEADS
    n_pair = nh * sub_chunks

    @pl.when(pl.program_id(1) == 0)
    def _():
        state_ref[...] = jnp.zeros_like(state_ref)
        xbuf_ref[:, 0:8, :] = jnp.zeros((3 * nh, 8, hd), F32)

    xbuf_ref[0:nh, 8:8 + tb, :] = q_ref[...].astype(F32)
    xbuf_ref[nh:2 * nh, 8:8 + tb, :] = k_ref[...].astype(F32)
    xbuf_ref[2 * nh:3 * nh, 8:8 + tb, :] = v_ref[...].astype(F32)

    def pairs(a):
        return a.reshape(n_pair, c, a.shape[-1])

    def unpair(a):
        return a.reshape(nh, sub_chunks, c, a.shape[-1])

    row2 = lax.broadcasted_iota(jnp.int32, (c, 2 * c), 0)
    col2 = lax.broadcasted_iota(jnp.int32, (c, 2 * c), 1) % c
    causal2 = row2 >= col2
    strict2 = row2 > col2

    def chunk_local(s):
        r0 = s * ts
        y = cw_ref[:, 0:1, :] * xbuf_ref[:, 5 + r0:5 + r0 + ts, :]
        for i in range(1, CONV_WIDTH):
            y = y + cw_ref[:, i:i + 1, :] * xbuf_ref[:, 5 + i + r0:5 + i + r0 + ts, :]
        y = _silu(y)
        q, k, v = y[0:nh], y[nh:2 * nh], y[2 * nh:3 * nh]
        qn = q * lax.rsqrt(jnp.sum(q * q, axis=-1, keepdims=True) + NORM_EPS) * (hd ** -0.5)
        kn = k * lax.rsqrt(jnp.sum(k * k, axis=-1, keepdims=True) + NORM_EPS)

        gates_blk = g_ref[r0:r0 + ts, :]
        lane = lax.broadcasted_iota(jnp.int32, gates_blk.shape, 1)

        def gate_cols(grp):
            return jnp.stack([jnp.sum(jnp.where(lane == grp * nh + h, gates_blk, 0.0), axis=-1, keepdims=True)
                              for h in range(nh)], axis=0)

        beta, gc, eg, edec = (gate_cols(grp) for grp in (_BETA, _GCUM, _EG, _EDEC))
        gr2 = gt_ref[:, s * sub_chunks:(s + 1) * sub_chunks].reshape(n_pair, 1, 2 * c)
        decay2 = jnp.where(causal2, jnp.exp(jnp.where(causal2, pairs(gc) - gr2, 0.0)), 0.0)
        kb = kn * beta
        kn_b = pairs(kn).astype(BF16)
        kkt2 = _bdot_nt(pairs(kb).astype(BF16), jnp.concatenate([kn_b, kn_b], axis=1)) * decay2
        inv2 = _unit_lower_inverse2(jnp.where(strict2, -kkt2, 0.0))
        rhs = pairs(jnp.concatenate([v * beta, kb * eg], axis=-1))
        sol = unpair(_bdot(_as_lhs(_split(inv2)), _as_rhs(_split(rhs))))
        qk = _bdot_nt(pairs(qn).astype(BF16), kn_b) * decay2[..., :c]
        return dict(u=sol[..., :hd], w=sol[..., hd:], qk=unpair(qk), qe=unpair(pairs(qn * eg)),
                    kd=unpair(pairs(kn * edec)), eg=unpair(pairs(eg)))

    local = [chunk_local(s) for s in range(n_sub)]
    xbuf_ref[:, 0:8, :] = xbuf_ref[:, tb:tb + 8, :]

    state = state_ref[...]
    for s in range(n_sub):
        part = local[s]
        for n in range(sub_chunks):
            r0 = s * ts + n * c
            state_b = state.astype(BF16)
            ws_qs = _bdot(jnp.concatenate([part["w"][:, n].astype(BF16), part["qe"][:, n].astype(BF16)], axis=1),
                          state_b)
            v_new = part["u"][:, n] - ws_qs[:, :c]
            v_new_b = v_new.astype(BF16)
            o_n = ws_qs[:, c:] + _bdot(part["qk"][:, n].astype(BF16), v_new_b)
            state = state * part["eg"][:, n, c - 1:c, :] + _bdot_tn(part["kd"][:, n].astype(BF16), v_new_b)
            out = _rms(o_n, nw_ref[...]) * _silu(z_ref[:, r0:r0 + c, :].astype(F32))
            for h in range(nh):
                o_ref[r0:r0 + c, h * hd:(h + 1) * hd] = out[h].astype(o_ref.dtype)
    state_ref[...] = state


def deltanet(proj_hm, gates_arr, gates_t, conv_w, dn_norm_w, layer, *, batch, tb=512, ts=256):
    _, m, _ = proj_hm.shape
    t_len = m // batch
    tb = _tile(t_len, tb)
    ts = _tile(tb, ts)
    n_t = t_len // tb
    hd = HEAD_DIM
    depth = conv_w.shape[0]
    cw = conv_w.reshape(depth, CONV_WIDTH, 3 * DN_HEADS, hd).transpose(0, 2, 1, 3)

    def grp(base):
        return pl.BlockSpec((DN_HEADS, tb, hd), lambda b, t: (base // DN_HEADS, b * n_t + t, 0))

    return pl.pallas_call(
        functools.partial(_deltanet_kernel, n_sub=tb // ts, sub_chunks=ts // DN_CHUNK),
        grid=(batch, n_t),
        in_specs=[grp(_Q_GRP), grp(_K_GRP), grp(_VV_GRP), grp(_Z_GRP),
                  pl.BlockSpec((tb, LANES), lambda b, t: (b * n_t + t, 0)),
                  pl.BlockSpec((DN_HEADS, tb // DN_CHUNK, 1, 2 * DN_CHUNK), lambda b, t: (0, b * n_t + t, 0, 0)),
                  pl.BlockSpec((None, 3 * DN_HEADS, CONV_WIDTH, hd), lambda b, t: (layer, 0, 0, 0)),
                  pl.BlockSpec((None, 1, hd), lambda b, t: (layer, 0, 0))],
        out_specs=pl.BlockSpec((tb, DN_HEADS * hd), lambda b, t: (b * n_t + t, 0)),
        out_shape=jax.ShapeDtypeStruct((m, DN_HEADS * hd), BF16),
        scratch_shapes=[pltpu.VMEM((DN_HEADS, hd, hd), F32),
                        pltpu.VMEM((3 * DN_HEADS, tb + 8, hd), F32)],
        compiler_params=_params("parallel", "arbitrary"),
        name="deltanet",
    )(proj_hm, proj_hm, proj_hm, proj_hm, gates_arr, gates_t, cw, dn_norm_w.reshape(depth, 1, hd))


def _xattn_fold_kernel(kv_ref, w_ref, o_ref, wb_ref, *, n_cast, fold_keys, head_dim, n_mem):
    step = pl.program_id(0)
    ck = w_ref.shape[0]

    @pl.when(step < n_cast)
    def _():
        wb_ref[pl.ds(pl.multiple_of(step * ck, ck), ck), :] = w_ref[...].astype(BF16)

    @pl.when(step >= n_cast)
    def _():
        for h in range(XA_HEADS):
            feat = slice(h * head_dim, (h + 1) * head_dim)
            toks = slice(h * n_mem, (h + 1) * n_mem)
            if fold_keys:
                o_ref[:, toks] = _dot_nt(wb_ref[:, feat], kv_ref[:, feat]).astype(o_ref.dtype)
            else:
                o_ref[toks, :] = jnp.dot(kv_ref[:, feat], wb_ref[feat, :],
                                         preferred_element_type=F32).astype(o_ref.dtype)


def xattn_fold(kv, w_stack, layer, *, batch, fold_keys, ck=512):
    n_mem = kv.shape[0] // batch
    d = w_stack.shape[1]
    ck = _tile(d, ck)
    n_cast = d // ck
    out_dims = (d, XA_HEADS * n_mem) if fold_keys else (XA_HEADS * n_mem, d)
    return pl.pallas_call(
        functools.partial(_xattn_fold_kernel, n_cast=n_cast, fold_keys=fold_keys, head_dim=d // XA_HEADS,
                          n_mem=n_mem),
        grid=(n_cast + batch,),
        in_specs=[pl.BlockSpec((n_mem, d), lambda i: (jnp.maximum(i - n_cast, 0), 0 if fold_keys else 1)),
                  pl.BlockSpec((None, ck, d), lambda i: (layer, jnp.minimum(i, n_cast - 1), 0))],
        out_specs=pl.BlockSpec((None,) + out_dims, lambda i: (jnp.maximum(i - n_cast, 0), 0, 0)),
        out_shape=jax.ShapeDtypeStruct((batch,) + out_dims, BF16),
        scratch_shapes=[pltpu.VMEM((d, d), BF16)],
        compiler_params=_params("arbitrary"),
        name="xattn_fold",
    )(kv, w_stack)


def _xattn_post_kernel(xn_ref, wk_ref, vw_ref, x_ref, gp_ref, gn_ref, x_out_ref, xn_out_ref, *,
                       n_sub, head_dim, n_mem):
    ts = x_ref.shape[0] // n_sub
    scale = head_dim ** -0.5
    for s in range(n_sub):
        rows = slice(s * ts, (s + 1) * ts)
        logits = jnp.dot(xn_ref[rows, :], wk_ref[...], preferred_element_type=F32) * scale
        probs = []
        for h in range(XA_HEADS):
            l_h = logits[:, h * n_mem:(h + 1) * n_mem]
            e = jnp.exp(l_h - jnp.max(l_h, axis=-1, keepdims=True))
            probs.append((e / jnp.sum(e, axis=-1, keepdims=True)).astype(BF16))
        c = jnp.dot(jnp.concatenate(probs, axis=-1), vw_ref[...], preferred_element_type=F32)
        x_new = x_ref[rows, :] + _rms(c, gp_ref[...])
        x_out_ref[rows, :] = x_new
        xn_out_ref[rows, :] = _rms(x_new, gn_ref[...]).astype(xn_out_ref.dtype)


def xattn_post(xn, wk, vw, x, g_post, g_next, *, batch, tm=512, n_sub=2):
    m, d = x.shape
    t_len = m // batch
    tm = _tile(t_len, tm)
    tiles_per_seq = t_len // tm
    n_mem = wk.shape[2] // XA_HEADS
    row = pl.BlockSpec((tm, d), lambda i: (i, 0))

    def vec(pair):
        stack, lyr = pair
        return stack.reshape(stack.shape[0], 1, d), pl.BlockSpec((None, 1, d), lambda i: (lyr, 0, 0))

    gp_arr, gp_spec = vec(g_post)
    gn_arr, gn_spec = vec(g_next)
    return pl.pallas_call(
        functools.partial(_xattn_post_kernel, n_sub=n_sub, head_dim=d // XA_HEADS, n_mem=n_mem),
        grid=(m // tm,),
        in_specs=[row,
                  pl.BlockSpec((None,) + wk.shape[1:], lambda i: (i // tiles_per_seq, 0, 0)),
                  pl.BlockSpec((None,) + vw.shape[1:], lambda i: (i // tiles_per_seq, 0, 0)),
                  row, gp_spec, gn_spec],
        out_specs=[row, row],
        out_shape=[jax.ShapeDtypeStruct((m, d), F32), jax.ShapeDtypeStruct((m, d), BF16)],
        compiler_params=_params("parallel"),
        name="xattn_post",
    )(xn, wk, vw, x, gp_arr, gn_arr)


def kernel(x, mem, ffn1_norm_pre, ffn1_w_gate_up, ffn1_w_down, ffn1_norm_post, mix_norm_pre, w_in, conv_w, a_log, dt_bias, sm_w, sm_b, sm_ln_g, sm_ln_b, dn_norm_w, w_out, mix_norm_post, xa_norm_pre, mem_norm, w_xq, w_xkv, w_xo, xa_norm_post, ffn2_norm_pre, ffn2_w_gate_up, ffn2_w_down, ffn2_norm_post):
    batch, t_len, d = x.shape
    depth = w_in.shape[0]
    d_ff = ffn1_w_down.shape[1]
    n_main = _N_GRP * LANES
    x2 = x.reshape(batch * t_len, d)
    mem2 = mem.reshape(-1, d)
    w_in_t = jnp.swapaxes(w_in, 1, 2)

    def ffn(xn, x2, w_gate_up, w_down, l, g_post, g_next):
        hmid, w_down_b = matmul_ws(xn, w_gate_up, l, mode="swiglu", n_cols=d_ff, col_offsets=(0, d_ff), tm=2048,
                                   side_cast=w_down)
        return matmul_post([hmid], w_down_b, l, x2, g_post, g_next, scale=0.5, tm=512, n_sub=4)

    xn = rmsnorm(x2, ffn1_norm_pre, 0)
    for l in range(depth):
        x2, xn = ffn(xn, x2, ffn1_w_gate_up, ffn1_w_down, l, (ffn1_norm_post, l), (mix_norm_pre, l))

        proj_hm = matmul_ws(xn, w_in_t, l, mode="headmajor", n_cols=n_main, tm=2048, tn=1024,
                            w_rows_are_outputs=True)
        gates_arr, gates_t = gates(xn, w_in_t, a_log, dt_bias, l)
        y_b = deltanet(proj_hm, gates_arr, gates_t, conv_w, dn_norm_w, l, batch=batch)
        x2, xn = mix_out(proj_hm, y_b, sm_w, sm_b, sm_ln_g, sm_ln_b, w_out, l, x2,
                         (mix_norm_post, l), (xa_norm_pre, l))

        kv = matmul_ws(rmsnorm(mem2, mem_norm, l), w_xkv, l)
        wk = xattn_fold(kv, w_xq, l, batch=batch, fold_keys=True)
        vw = xattn_fold(kv, w_xo, l, batch=batch, fold_keys=False)
        x2, xn = xattn_post(xn, wk, vw, x2, (xa_norm_post, l), (ffn2_norm_pre, l), batch=batch)

        g_next = (ffn1_norm_pre, l + 1) if l + 1 < depth else None
        x2, xn = ffn(xn, x2, ffn2_w_gate_up, ffn2_w_down, l, (ffn2_norm_post, l), g_next)
    return x2.reshape(batch, t_len, d)
```

```python
import functools

import jax
import jax.numpy as jnp
from jax import lax
from jax.experimental import pallas as pl
from jax.experimental.pallas import tpu as pltpu

F32 = jnp.float32
BF16 = jnp.bfloat16

NORM_EPS = 1e-6
HEAD_DIM = 128
GM_HEADS = 8
DN_HEADS = 8
GM_CHUNK = 128
DN_CHUNK = 64
CONV_WIDTH = 4
XA_HEADS = 4
LANES = 128
VMEM_LIMIT_BYTES = 58 * 1024 * 1024

_U_GRP, _V_GRP, _Q_GRP, _K_GRP, _VV_GRP, _Z_GRP = 0, 8, 16, 24, 32, 40
_N_GRP = 48
_BETA, _GCUM, _EG, _EDEC = 0, 1, 2, 3
_N_DECAY_COPIES = 3


def _params(*sem):
    return pltpu.CompilerParams(dimension_semantics=sem, vmem_limit_bytes=VMEM_LIMIT_BYTES)


def _tile(dim, want):
    t = min(dim, want)
    assert dim % t == 0, (dim, want)
    return t


def _rms(x, g):
    return x * lax.rsqrt(jnp.mean(x * x, axis=-1, keepdims=True) + NORM_EPS) * g


def _silu(x):
    return x * jax.nn.sigmoid(x)


def _gelu(x):
    return 0.5 * x * (1.0 + lax.erf(x * (2.0 ** -0.5)))


def _rmsnorm_kernel(x_ref, g_ref, o_ref):
    o_ref[...] = _rms(x_ref[...], g_ref[...]).astype(o_ref.dtype)


def rmsnorm(x, g_stack, layer, out_dtype=BF16, tm=512):
    m, d = x.shape
    tm = _tile(m, tm)
    return pl.pallas_call(
        _rmsnorm_kernel,
        grid=(m // tm,),
        in_specs=[pl.BlockSpec((tm, d), lambda i: (i, 0)),
                  pl.BlockSpec((None, 1, d), lambda i: (layer, 0, 0))],
        out_specs=pl.BlockSpec((tm, d), lambda i: (i, 0)),
        out_shape=jax.ShapeDtypeStruct((m, d), out_dtype),
        compiler_params=_params("parallel"),
        name="rmsnorm",
    )(x, g_stack.reshape(g_stack.shape[0], 1, d))


_ROWS_PER_DOT = 1024


def _mm_ws_kernel(*refs, n_w, mode, w_rows_are_outputs, has_side_cast):
    x_ref = refs[0]
    w_refs = refs[1:1 + n_w]
    if has_side_cast:
        side_ref, o_ref, side_out_ref, wb_ref = refs[1 + n_w:]
    else:
        o_ref, wb_ref = refs[1 + n_w:]
    tn = wb_ref.shape[1] // n_w

    @pl.when(pl.program_id(1) == 0)
    def _():
        for i, w_ref in enumerate(w_refs):
            w = w_ref[...]
            wb_ref[:, i * tn:(i + 1) * tn] = (w.T if w_rows_are_outputs else w).astype(BF16)
        if has_side_cast:
            side_out_ref[...] = side_ref[...].astype(BF16)

    tm = x_ref.shape[0]
    for r0 in range(0, tm, min(tm, _ROWS_PER_DOT)):
        rows = slice(r0, r0 + min(tm, _ROWS_PER_DOT))
        y = jnp.dot(x_ref[rows, :], wb_ref[...], preferred_element_type=F32)
        if mode == "swiglu":
            o_ref[rows, :] = (_silu(y[:, :tn]) * y[:, tn:]).astype(o_ref.dtype)
        elif mode == "headmajor":
            for c in range(tn // LANES):
                o_ref[c, rows, :] = y[:, c * LANES:(c + 1) * LANES].astype(o_ref.dtype)
        else:
            o_ref[rows, :] = y.astype(o_ref.dtype)


def matmul_ws(x, w_stack, layer, *, mode="plain", out_dtype=BF16, tm=1024, tn=512, n_cols=None, col_offsets=(0,),
              w_rows_are_outputs=False, side_cast=None):
    m, k = x.shape
    n = w_stack.shape[1 if w_rows_are_outputs else 2] if n_cols is None else n_cols
    tm, tn = _tile(m, tm), _tile(n, tn)
    n_w = len(col_offsets)
    assert all(off % tn == 0 for off in col_offsets)
    n_col_tiles = n // tn

    def w_spec(off):
        if w_rows_are_outputs:
            return pl.BlockSpec((None, tn, k), lambda j, i: (layer, j + off // tn, 0))
        return pl.BlockSpec((None, k, tn), lambda j, i: (layer, 0, j + off // tn))

    if mode == "headmajor":
        out_shape = jax.ShapeDtypeStruct((n // LANES, m, LANES), out_dtype)
        out_spec = pl.BlockSpec((tn // LANES, tm, LANES), lambda j, i: (j, i, 0))
    else:
        out_shape = jax.ShapeDtypeStruct((m, n), out_dtype)
        out_spec = pl.BlockSpec((tm, tn), lambda j, i: (i, j))
    in_specs = [pl.BlockSpec((tm, k), lambda j, i: (i, 0))] + [w_spec(off) for off in col_offsets]
    args = [x] + [w_stack] * n_w
    if side_cast is not None:
        _, side_rows, side_cols = side_cast.shape
        chunk = side_rows // n_col_tiles
        assert chunk * n_col_tiles == side_rows
        in_specs.append(pl.BlockSpec((None, chunk, side_cols), lambda j, i: (layer, j, 0)))
        args.append(side_cast)
        out_spec = [out_spec, pl.BlockSpec((chunk, side_cols), lambda j, i: (j, 0))]
        out_shape = [out_shape, jax.ShapeDtypeStruct((side_rows, side_cols), BF16)]
    return pl.pallas_call(
        functools.partial(_mm_ws_kernel, n_w=n_w, mode=mode, w_rows_are_outputs=w_rows_are_outputs,
                          has_side_cast=side_cast is not None),
        grid=(n_col_tiles, m // tm),
        in_specs=in_specs,
        out_specs=out_spec,
        out_shape=out_shape,
        scratch_shapes=[pltpu.VMEM((k, n_w * tn), BF16)],
        compiler_params=_params("arbitrary", "arbitrary"),
        name="mm_ws_" + mode,
    )(*args)


def _mm_post_kernel(a_ref, w_ref, x_ref, gp_ref, *rest, n_sub, scale, has_next):
    if has_next:
        gn_ref, x_out_ref, xn_out_ref = rest
    else:
        gn_ref = xn_out_ref = None
        (x_out_ref,) = rest
    ts = x_ref.shape[0] // n_sub
    for s in range(n_sub):
        rows = slice(s * ts, (s + 1) * ts)
        f = jnp.dot(a_ref[rows, :], w_ref[...], preferred_element_type=F32)
        x_new = x_ref[rows, :] + scale * _rms(f, gp_ref[...])
        x_out_ref[rows, :] = x_new
        if has_next:
            xn_out_ref[rows, :] = _rms(x_new, gn_ref[...]).astype(xn_out_ref.dtype)


def matmul_post(act, w_b, x, g_post, g_next, *, scale, tm, n_sub):
    m, d = x.shape
    k = act.shape[1]
    tm = _tile(m, tm)
    has_next = g_next is not None
    assert w_b.shape == (k, d) and w_b.dtype == BF16
    row = pl.BlockSpec((tm, d), lambda i: (i, 0))

    def vec(pair):
        stack, lyr = pair
        return stack.reshape(stack.shape[0], 1, d), pl.BlockSpec((None, 1, d), lambda i: (lyr, 0, 0))

    gp_arr, gp_spec = vec(g_post)
    in_specs = [pl.BlockSpec((tm, k), lambda i: (i, 0)),
                pl.BlockSpec((k, d), lambda i: (0, 0), pipeline_mode=pl.Buffered(1)),
                row, gp_spec]
    args = [act, w_b, x, gp_arr]
    out_shape = [jax.ShapeDtypeStruct((m, d), F32)]
    out_specs = [row]
    if has_next:
        gn_arr, gn_spec = vec(g_next)
        in_specs.append(gn_spec)
        args.append(gn_arr)
        out_shape.append(jax.ShapeDtypeStruct((m, d), BF16))
        out_specs.append(row)
    outs = pl.pallas_call(
        functools.partial(_mm_post_kernel, n_sub=n_sub, scale=scale, has_next=has_next),
        grid=(m // tm,),
        in_specs=in_specs,
        out_specs=out_specs,
        out_shape=out_shape,
        compiler_params=_params("parallel"),
        name="mm_post",
    )(*args)
    return (outs[0], outs[1]) if has_next else (outs[0], None)


def _sgu_rows(u_ref, v_ref, w_ref, bt_ref, lg_ref, lb_ref, r0, n_chunks):
    c = GM_CHUNK
    rows = slice(r0, r0 + n_chunks * c)
    width = GM_HEADS * HEAD_DIM
    row = lax.broadcasted_iota(jnp.int32, (c, c), 0)
    col = lax.broadcasted_iota(jnp.int32, (c, c), 1)
    causal = row >= col
    vg = [_gelu(v_ref[h, rows, :].astype(F32)) for h in range(GM_HEADS)]
    mu = jnp.sum(sum(vg), axis=-1, keepdims=True) * (1.0 / width)
    var = jnp.sum(sum(jnp.square(x - mu) for x in vg), axis=-1, keepdims=True) * (1.0 / width)
    rstd = lax.rsqrt(var + NORM_EPS)
    heads = []
    for h in range(GM_HEADS):
        vn = ((vg[h] - mu) * rstd * lg_ref[h] + lb_ref[h]).astype(BF16)
        w = jnp.where(causal, w_ref[h], 0.0).astype(BF16)
        rhs = jnp.concatenate([vn[n * c:(n + 1) * c] for n in range(n_chunks)], axis=1)
        mixed = jnp.dot(w, rhs, preferred_element_type=F32) + bt_ref[:, h:h + 1]
        mixed = jnp.concatenate([mixed[:, n * HEAD_DIM:(n + 1) * HEAD_DIM] for n in range(n_chunks)], axis=0)
        heads.append((_gelu(u_ref[h, rows, :].astype(F32)) * mixed).astype(BF16))
    return jnp.concatenate(heads, axis=-1)


def _mix_out_kernel(u_ref, v_ref, sw_ref, bt_ref, lg_ref, lb_ref, yb_ref, w_ref, x_ref, gp_ref, gn_ref,
                    x_out_ref, xn_out_ref, wb_ref, *, n_cast, n_sub):
    step = pl.program_id(0)
    ck = w_ref.shape[0]
    ka = GM_HEADS * HEAD_DIM

    @pl.when(step < n_cast)
    def _():
        wb_ref[pl.ds(pl.multiple_of(step * ck, ck), ck), :] = w_ref[...].astype(BF16)

    @pl.when(step >= n_cast)
    def _():
        ts = x_ref.shape[0] // n_sub
        for s in range(n_sub):
            rows = slice(s * ts, (s + 1) * ts)
            y_a = _sgu_rows(u_ref, v_ref, sw_ref, bt_ref, lg_ref, lb_ref, s * ts, ts // GM_CHUNK)
            f = (jnp.dot(y_a, wb_ref[0:ka, :], preferred_element_type=F32)
                 + jnp.dot(yb_ref[rows, :], wb_ref[ka:, :], preferred_element_type=F32))
            x_new = x_ref[rows, :] + _rms(f, gp_ref[...])
            x_out_ref[rows, :] = x_new
            xn_out_ref[rows, :] = _rms(x_new, gn_ref[...]).astype(xn_out_ref.dtype)


def mix_out(proj_hm, y_b, sm_w, sm_b, ln_g, ln_b, w_out, layer, x, g_post, g_next, *, tm=512, n_sub=4, ck=512):
    m, d = x.shape
    tm = _tile(m, tm)
    depth = sm_w.shape[0]
    k_all = w_out.shape[1]
    ck = _tile(k_all, ck)
    n_cast = k_all // ck

    def tile_idx(i):
        return jnp.maximum(i - n_cast, 0)

    row = pl.BlockSpec((tm, d), lambda i: (tile_idx(i), 0))

    def vec(pair):
        stack, lyr = pair
        return stack.reshape(stack.shape[0], 1, d), pl.BlockSpec((None, 1, d), lambda i: (lyr, 0, 0))

    gp_arr, gp_spec = vec(g_post)
    gn_arr, gn_spec = vec(g_next)
    outs = pl.pallas_call(
        functools.partial(_mix_out_kernel, n_cast=n_cast, n_sub=n_sub),
        grid=(n_cast + m // tm,),
        in_specs=[pl.BlockSpec((GM_HEADS, tm, HEAD_DIM), lambda i: (_U_GRP // GM_HEADS, tile_idx(i), 0)),
                  pl.BlockSpec((GM_HEADS, tm, HEAD_DIM), lambda i: (_V_GRP // GM_HEADS, tile_idx(i), 0)),
                  pl.BlockSpec((None, GM_HEADS, GM_CHUNK, GM_CHUNK), lambda i: (layer, 0, 0, 0)),
                  pl.BlockSpec((None, GM_CHUNK, GM_HEADS), lambda i: (layer, 0, 0)),
                  pl.BlockSpec((None, GM_HEADS, 1, HEAD_DIM), lambda i: (layer, 0, 0, 0)),
                  pl.BlockSpec((None, GM_HEADS, 1, HEAD_DIM), lambda i: (layer, 0, 0, 0)),
                  pl.BlockSpec((tm, y_b.shape[1]), lambda i: (tile_idx(i), 0)),
                  pl.BlockSpec((None, ck, d), lambda i: (layer, jnp.minimum(i, n_cast - 1), 0)),
                  row, gp_spec, gn_spec],
        out_specs=[row, row],
        out_shape=[jax.ShapeDtypeStruct((m, d), F32), jax.ShapeDtypeStruct((m, d), BF16)],
        scratch_shapes=[pltpu.VMEM((k_all, d), BF16)],
        compiler_params=_params("arbitrary"),
        name="mix_out",
    )(proj_hm, proj_hm, sm_w, sm_b.transpose(0, 2, 1),
      ln_g.reshape(depth, GM_HEADS, 1, HEAD_DIM), ln_b.reshape(depth, GM_HEADS, 1, HEAD_DIM),
      y_b, w_out, x, gp_arr, gn_arr)
    return outs[0], outs[1]


def _gates_kernel(x_ref, w_ref, alog_ref, dtb_ref, g_ref, gt_ref, wb_ref, *, n_chunks):
    c = DN_CHUNK

    @pl.when(pl.program_id(0) == 0)
    def _():
        w = w_ref[...].T
        lane_w = lax.broadcasted_iota(jnp.int32, w.shape, 1)
        w_valid = jnp.where(lane_w < 2 * DN_HEADS, w, 0.0)
        w_decay = jnp.where(lane_w >= DN_HEADS, w_valid, 0.0)
        w_all = w_valid
        for r in range(1, _N_DECAY_COPIES):
            w_all = w_all + pltpu.roll(w_decay, r * DN_HEADS, axis=1)
        wb_ref[...] = w_all.astype(BF16)

    ba = jnp.dot(x_ref[...], wb_ref[...], preferred_element_type=F32)
    grp = lax.broadcasted_iota(jnp.int32, ba.shape, 1) // DN_HEADS
    beta = jax.nn.sigmoid(ba)
    g = -jnp.exp(alog_ref[...]) * jax.nn.softplus(ba + dtb_ref[...])
    row = lax.broadcasted_iota(jnp.int32, (c, c), 0)
    col = lax.broadcasted_iota(jnp.int32, (c, c), 1)
    tri = (row >= col).astype(F32)
    gcum_chunks, last_chunks = [], []
    for n in range(n_chunks):
        gc = jnp.dot(tri, g[n * c:(n + 1) * c], preferred_element_type=F32, precision=lax.Precision.HIGHEST)
        gcum_chunks.append(gc)
        last_chunks.append(jnp.broadcast_to(gc[c - 1:c, :], gc.shape))
    gcum = jnp.concatenate(gcum_chunks, axis=0)
    glast = jnp.concatenate(last_chunks, axis=0)
    out = jnp.where(grp == _BETA, beta,
                    jnp.where(grp == _GCUM, gcum,
                              jnp.where(grp == _EG, jnp.exp(gcum),
                                        jnp.where(grp == _EDEC, jnp.exp(glast - gcum), 0.0))))
    g_ref[...] = out
    for n in range(n_chunks):
        twice_t = jnp.concatenate([gcum_chunks[n], gcum_chunks[n]], axis=0).T
        for h in range(DN_HEADS):
            lane_idx = _GCUM * DN_HEADS + h
            gt_ref[h, n] = twice_t[lane_idx:lane_idx + 1, :]


def gates(xn, w_in_t, a_log, dt_bias, layer, *, tb=256):
    m, k = xn.shape
    tb = _tile(m, tb)
    depth = a_log.shape[0]
    tail_block = (w_in_t.shape[1] - 2 * DN_HEADS) // LANES
    assert tail_block * LANES + 2 * DN_HEADS == w_in_t.shape[1]

    def lane_row(p):
        z = jnp.zeros((depth, 1, DN_HEADS), F32)
        r = p.reshape(depth, 1, DN_HEADS)
        tail = jnp.zeros((depth, 1, LANES - (1 + _N_DECAY_COPIES) * DN_HEADS), F32)
        return jnp.concatenate([z] + [r] * _N_DECAY_COPIES + [tail], axis=-1)

    return pl.pallas_call(
        functools.partial(_gates_kernel, n_chunks=tb // DN_CHUNK),
        grid=(m // tb,),
        in_specs=[pl.BlockSpec((tb, k), lambda i: (i, 0)),
                  pl.BlockSpec((None, LANES, k), lambda i: (layer, tail_block, 0)),
                  pl.BlockSpec((None, 1, LANES), lambda i: (layer, 0, 0)),
                  pl.BlockSpec((None, 1, LANES), lambda i: (layer, 0, 0))],
        out_specs=[pl.BlockSpec((tb, LANES), lambda i: (i, 0)),
                   pl.BlockSpec((DN_HEADS, tb // DN_CHUNK, 1, 2 * DN_CHUNK), lambda i: (0, i, 0, 0))],
        out_shape=[jax.ShapeDtypeStruct((m, LANES), F32),
                   jax.ShapeDtypeStruct((DN_HEADS, m // DN_CHUNK, 1, 2 * DN_CHUNK), F32)],
        scratch_shapes=[pltpu.VMEM((k, LANES), BF16)],
        compiler_params=_params("arbitrary"),
        name="dn_gates",
    )(xn, w_in_t, lane_row(a_log), lane_row(dt_bias))


def _dot_nt(a, b):
    return lax.dot_general(a, b, (((1,), (1,)), ((), ())), preferred_element_type=F32)


def _bdot(a, b):
    return lax.dot_general(a, b, (((2,), (1,)), ((0,), (0,))), preferred_element_type=F32)


def _bdot_nt(a, b):
    return lax.dot_general(a, b, (((2,), (2,)), ((0,), (0,))), preferred_element_type=F32)


def _bdot_tn(a, b):
    return lax.dot_general(a, b, (((1,), (1,)), ((0,), (0,))), preferred_element_type=F32)


def _split(x):
    hi = x.astype(BF16)
    lo = (x - hi.astype(F32)).astype(BF16)
    return hi, lo


def _as_lhs(parts):
    hi, lo = parts
    return jnp.concatenate([hi, lo], axis=-1)


def _as_rhs(parts):
    hi, lo = parts
    return jnp.concatenate([hi, lo, hi, lo], axis=-2)


def _unit_lower_inverse2(n2):
    c = n2.shape[-2]
    row = lax.broadcasted_iota(jnp.int32, n2.shape[-2:], 0)
    col = lax.broadcasted_iota(jnp.int32, n2.shape[-2:], 1)
    eye2 = (row == (col % c)).astype(F32)
    inv = eye2 + n2
    parts = _split(n2)
    span = 2
    while span < c:
        n2 = _bdot(_as_lhs(parts), _as_rhs(parts))
        parts = _split(n2)
        inv = inv + _bdot(_as_lhs(_split(inv)), _as_rhs(parts))
        span *= 2
    return inv


def _deltanet_kernel(q_ref, k_ref, v_ref, z_ref, g_ref, gt_ref, cw_ref, nw_ref,
                     o_ref, state_ref, xbuf_ref, *, n_sub, sub_chunks):
    c = DN_CHUNK
    ts = sub_chunks * c
    tb = n_sub * ts
    hd = HEAD_DIM
    nh = DN_HEADS
    n_pair = nh * sub_chunks

    @pl.when(pl.program_id(1) == 0)
    def _():
        state_ref[...] = jnp.zeros_like(state_ref)
        xbuf_ref[:, 0:8, :] = jnp.zeros((3 * nh, 8, hd), F32)

    xbuf_ref[0:nh, 8:8 + tb, :] = q_ref[...].astype(F32)
    xbuf_ref[nh:2 * nh, 8:8 + tb, :] = k_ref[...].astype(F32)
    xbuf_ref[2 * nh:3 * nh, 8:8 + tb, :] = v_ref[...].astype(F32)

    def pairs(a):
        return a.reshape(n_pair, c, a.shape[-1])

    def unpair(a):
        return a.reshape(nh, sub_chunks, c, a.shape[-1])

    row2 = lax.broadcasted_iota(jnp.int32, (c, 2 * c), 0)
    col2 = lax.broadcasted_iota(jnp.int32, (c, 2 * c), 1) % c
    causal2 = row2 >= col2
    strict2 = row2 > col2

    def chunk_local(s):
        r0 = s * ts
        y = cw_ref[:, 0:1, :] * xbuf_ref[:, 5 + r0:5 + r0 + ts, :]
        for i in range(1, CONV_WIDTH):
            y = y + cw_ref[:, i:i + 1, :] * xbuf_ref[:, 5 + i + r0:5 + i + r0 + ts, :]
        y = _silu(y)
        q, k, v = y[0:nh], y[nh:2 * nh], y[2 * nh:3 * nh]
        qn = q * lax.rsqrt(jnp.sum(q * q, axis=-1, keepdims=True) + NORM_EPS) * (hd ** -0.5)
        kn = k * lax.rsqrt(jnp.sum(k * k, axis=-1, keepdims=True) + NORM_EPS)

        gates_blk = g_ref[r0:r0 + ts, :]
        lane = lax.broadcasted_iota(jnp.int32, gates_blk.shape, 1)

        def gate_cols(grp):
            return jnp.stack([jnp.sum(jnp.where(lane == grp * nh + h, gates_blk, 0.0), axis=-1, keepdims=True)
                              for h in range(nh)], axis=0)

        beta, gc, eg, edec = (gate_cols(grp) for grp in (_BETA, _GCUM, _EG, _EDEC))
        gr2 = gt_ref[:, s * sub_chunks:(s + 1) * sub_chunks].reshape(n_pair, 1, 2 * c)
        decay2 = jnp.where(causal2, jnp.exp(jnp.where(causal2, pairs(gc) - gr2, 0.0)), 0.0)
        kb = kn * beta
        kn_b = pairs(kn).astype(BF16)
        kkt2 = _bdot_nt(pairs(kb).astype(BF16), jnp.concatenate([kn_b, kn_b], axis=1)) * decay2
        inv2 = _unit_lower_inverse2(jnp.where(strict2, -kkt2, 0.0))
        rhs = pairs(jnp.concatenate([v * beta, kb * eg], axis=-1))
        sol = unpair(_bdot(_as_lhs(_split(inv2)), _as_rhs(_split(rhs))))
        qk = _bdot_nt(pairs(qn).astype(BF16), kn_b) * decay2[..., :c]
        return dict(u=sol[..., :hd], w=sol[..., hd:], qk=unpair(qk), qe=unpair(pairs(qn * eg)),
                    kd=unpair(pairs(kn * edec)), eg=unpair(pairs(eg)))

    local = [chunk_local(s) for s in range(n_sub)]
    xbuf_ref[:, 0:8, :] = xbuf_ref[:, tb:tb + 8, :]

    state = state_ref[...]
    for s in range(n_sub):
        part = local[s]
        for n in range(sub_chunks):
            r0 = s * ts + n * c
            state_b = state.astype(BF16)
            ws_qs = _bdot(jnp.concatenate([part["w"][:, n].astype(BF16), part["qe"][:, n].astype(BF16)], axis=1),
                          state_b)
            v_new = part["u"][:, n] - ws_qs[:, :c]
            v_new_b = v_new.astype(BF16)
            o_n = ws_qs[:, c:] + _bdot(part["qk"][:, n].astype(BF16), v_new_b)
            state = state * part["eg"][:, n, c - 1:c, :] + _bdot_tn(part["kd"][:, n].astype(BF16), v_new_b)
            out = _rms(o_n, nw_ref[...]) * _silu(z_ref[:, r0:r0 + c, :].astype(F32))
            for h in range(nh):
                o_ref[r0:r0 + c, h * hd:(h + 1) * hd] = out[h].astype(o_ref.dtype)
    state_ref[...] = state


def deltanet(proj_hm, gates_arr, gates_t, conv_w, dn_norm_w, layer, *, batch, tb=512, ts=256):
    _, m, _ = proj_hm.shape
    t_len = m // batch
    tb = _tile(t_len, tb)
    ts = _tile(tb, ts)
    n_t = t_len // tb
    hd = HEAD_DIM
    depth = conv_w.shape[0]
    cw = conv_w.reshape(depth, CONV_WIDTH, 3 * DN_HEADS, hd).transpose(0, 2, 1, 3)

    def grp(base):
        return pl.BlockSpec((DN_HEADS, tb, hd), lambda b, t: (base // DN_HEADS, b * n_t + t, 0))

    return pl.pallas_call(
        functools.partial(_deltanet_kernel, n_sub=tb // ts, sub_chunks=ts // DN_CHUNK),
        grid=(batch, n_t),
        in_specs=[grp(_Q_GRP), grp(_K_GRP), grp(_VV_GRP), grp(_Z_GRP),
                  pl.BlockSpec((tb, LANES), lambda b, t: (b * n_t + t, 0)),
                  pl.BlockSpec((DN_HEADS, tb // DN_CHUNK, 1, 2 * DN_CHUNK), lambda b, t: (0, b * n_t + t, 0, 0)),
                  pl.BlockSpec((None, 3 * DN_HEADS, CONV_WIDTH, hd), lambda b, t: (layer, 0, 0, 0)),
                  pl.BlockSpec((None, 1, hd), lambda b, t: (layer, 0, 0))],
        out_specs=pl.BlockSpec((tb, DN_HEADS * hd), lambda b, t: (b * n_t + t, 0)),
        out_shape=jax.ShapeDtypeStruct((m, DN_HEADS * hd), BF16),
        scratch_shapes=[pltpu.VMEM((DN_HEADS, hd, hd), F32),
                        pltpu.VMEM((3 * DN_HEADS, tb + 8, hd), F32)],
        compiler_params=_params("parallel", "arbitrary"),
        name="deltanet",
    )(proj_hm, proj_hm, proj_hm, proj_hm, gates_arr, gates_t, cw, dn_norm_w.reshape(depth, 1, hd))


def _xattn_fold_kernel(kv_ref, w_ref, o_ref, wb_ref, *, n_cast, fold_keys, head_dim, n_mem):
    step = pl.program_id(0)
    ck = w_ref.shape[0]

    @pl.when(step < n_cast)
    def _():
        wb_ref[pl.ds(pl.multiple_of(step * ck, ck), ck), :] = w_ref[...].astype(BF16)

    @pl.when(step >= n_cast)
    def _():
        for h in range(XA_HEADS):
            feat = slice(h * head_dim, (h + 1) * head_dim)
            toks = slice(h * n_mem, (h + 1) * n_mem)
            if fold_keys:
                o_ref[:, toks] = _dot_nt(wb_ref[:, feat], kv_ref[:, feat]).astype(o_ref.dtype)
            else:
                o_ref[toks, :] = jnp.dot(kv_ref[:, feat], wb_ref[feat, :],
                                         preferred_element_type=F32).astype(o_ref.dtype)


def xattn_fold(kv, w_stack, layer, *, batch, fold_keys, ck=512):
    n_mem = kv.shape[0] // batch
    d = w_stack.shape[1]
    ck = _tile(d, ck)
    n_cast = d // ck
    out_dims = (d, XA_HEADS * n_mem) if fold_keys else (XA_HEADS * n_mem, d)
    return pl.pallas_call(
        functools.partial(_xattn_fold_kernel, n_cast=n_cast, fold_keys=fold_keys, head_dim=d // XA_HEADS,
                          n_mem=n_mem),
        grid=(n_cast + batch,),
        in_specs=[pl.BlockSpec((n_mem, d), lambda i: (jnp.maximum(i - n_cast, 0), 0 if fold_keys else 1)),
                  pl.BlockSpec((None, ck, d), lambda i: (layer, jnp.minimum(i, n_cast - 1), 0))],
        out_specs=pl.BlockSpec((None,) + out_dims, lambda i: (jnp.maximum(i - n_cast, 0), 0, 0)),
        out_shape=jax.ShapeDtypeStruct((batch,) + out_dims, BF16),
        scratch_shapes=[pltpu.VMEM((d, d), BF16)],
        compiler_params=_params("arbitrary"),
        name="xattn_fold",
    )(kv, w_stack)


def _xattn_post_kernel(xn_ref, wk_ref, vw_ref, x_ref, gp_ref, gn_ref, x_out_ref, xn_out_ref, *,
                       n_sub, head_dim, n_mem):
    ts = x_ref.shape[0] // n_sub
    scale = head_dim ** -0.5
    for s in range(n_sub):
        rows = slice(s * ts, (s + 1) * ts)
        logits = jnp.dot(xn_ref[rows, :], wk_ref[...], preferred_element_type=F32) * scale
        probs = []
        for h in range(XA_HEADS):
            l_h = logits[:, h * n_mem:(h + 1) * n_mem]
            e = jnp.exp(l_h - jnp.max(l_h, axis=-1, keepdims=True))
            probs.append((e / jnp.sum(e, axis=-1, keepdims=True)).astype(BF16))
        c = jnp.dot(jnp.concatenate(probs, axis=-1), vw_ref[...], preferred_element_type=F32)
        x_new = x_ref[rows, :] + _rms(c, gp_ref[...])
        x_out_ref[rows, :] = x_new
        xn_out_ref[rows, :] = _rms(x_new, gn_ref[...]).astype(xn_out_ref.dtype)


def xattn_post(xn, wk, vw, x, g_post, g_next, *, batch, tm=512, n_sub=2):
    m, d = x.shape
    t_len = m // batch
    tm = _tile(t_len, tm)
    tiles_per_seq = t_len // tm
    n_mem = wk.shape[2] // XA_HEADS
    row = pl.BlockSpec((tm, d), lambda i: (i, 0))

    def vec(pair):
        stack, lyr = pair
        return stack.reshape(stack.shape[0], 1, d), pl.BlockSpec((None, 1, d), lambda i: (lyr, 0, 0))

    gp_arr, gp_spec = vec(g_post)
    gn_arr, gn_spec = vec(g_next)
    return pl.pallas_call(
        functools.partial(_xattn_post_kernel, n_sub=n_sub, head_dim=d // XA_HEADS, n_mem=n_mem),
        grid=(m // tm,),
        in_specs=[row,
                  pl.BlockSpec((None,) + wk.shape[1:], lambda i: (i // tiles_per_seq, 0, 0)),
                  pl.BlockSpec((None,) + vw.shape[1:], lambda i: (i // tiles_per_seq, 0, 0)),
                  row, gp_spec, gn_spec],
        out_specs=[row, row],
        out_shape=[jax.ShapeDtypeStruct((m, d), F32), jax.ShapeDtypeStruct((m, d), BF16)],
        compiler_params=_params("parallel"),
        name="xattn_post",
    )(xn, wk, vw, x, gp_arr, gn_arr)


def kernel(x, mem, ffn1_norm_pre, ffn1_w_gate_up, ffn1_w_down, ffn1_norm_post, mix_norm_pre, w_in, conv_w, a_log, dt_bias, sm_w, sm_b, sm_ln_g, sm_ln_b, dn_norm_w, w_out, mix_norm_post, xa_norm_pre, mem_norm, w_xq, w_xkv, w_xo, xa_norm_post, ffn2_norm_pre, ffn2_w_gate_up, ffn2_w_down, ffn2_norm_post):
    batch, t_len, d = x.shape
    depth = w_in.shape[0]
    d_ff = ffn1_w_down.shape[1]
    n_main = _N_GRP * LANES
    x2 = x.reshape(batch * t_len, d)
    mem2 = mem.reshape(-1, d)
    w_in_t = jnp.swapaxes(w_in, 1, 2)

    def ffn(xn, x2, w_gate_up, w_down, l, g_post, g_next):
        hmid, w_down_b = matmul_ws(xn, w_gate_up, l, mode="swiglu", n_cols=d_ff, col_offsets=(0, d_ff), tm=2048,
                                   side_cast=w_down)
        return matmul_post(hmid, w_down_b, x2, g_post, g_next, scale=0.5, tm=512, n_sub=4)

    xn = rmsnorm(x2, ffn1_norm_pre, 0)
    for l in range(depth):
        x2, xn = ffn(xn, x2, ffn1_w_gate_up, ffn1_w_down, l, (ffn1_norm_post, l), (mix_norm_pre, l))

        proj_hm = matmul_ws(xn, w_in_t, l, mode="headmajor", n_cols=n_main, tm=2048, tn=1024,
                            w_rows_are_outputs=True)
        gates_arr, gates_t = gates(xn, w_in_t, a_log, dt_bias, l)
        y_b = deltanet(proj_hm, gates_arr, gates_t, conv_w, dn_norm_w, l, batch=batch)
        x2, xn = mix_out(proj_hm, y_b, sm_w, sm_b, sm_ln_g, sm_ln_b, w_out, l, x2,
                         (mix_norm_post, l), (xa_norm_pre, l))

        kv = matmul_ws(rmsnorm(mem2, mem_norm, l), w_xkv, l)
        wk = xattn_fold(kv, w_xq, l, batch=batch, fold_keys=True)
        vw = xattn_fold(kv, w_xo, l, batch=batch, fold_keys=False)
        x2, xn = xattn_post(xn, wk, vw, x2, (xa_norm_post, l), (ffn2_norm_pre, l), batch=batch)

        g_next = (ffn1_norm_pre, l + 1) if l + 1 < depth else None
        x2, xn = ffn(xn, x2, ffn2_w_gate_up, ffn2_w_down, l, (ffn2_norm_post, l), g_next)
    return x2.reshape(batch, t_len, d)
```

```python
import functools

import jax
import jax.numpy as jnp
from jax import lax
from jax.experimental import pallas as pl
from jax.experimental.pallas import tpu as pltpu

F32 = jnp.float32
BF16 = jnp.bfloat16

NORM_EPS = 1e-6
HEAD_DIM = 128
GM_HEADS = 8
DN_HEADS = 8
GM_CHUNK = 128
DN_CHUNK = 64
CONV_WIDTH = 4
XA_HEADS = 4
LANES = 128
VMEM_LIMIT_BYTES = 58 * 1024 * 1024

_U_GRP, _V_GRP, _Q_GRP, _K_GRP, _VV_GRP, _Z_GRP = 0, 8, 16, 24, 32, 40
_N_GRP = 48
_BETA, _GCUM, _EG, _EDEC = 0, 1, 2, 3
_N_DECAY_COPIES = 3


def _params(*sem):
    return pltpu.CompilerParams(dimension_semantics=sem, vmem_limit_bytes=VMEM_LIMIT_BYTES)


def _tile(dim, want):
    t = min(dim, want)
    assert dim % t == 0, (dim, want)
    return t


def _rms(x, g):
    return x * lax.rsqrt(jnp.mean(x * x, axis=-1, keepdims=True) + NORM_EPS) * g


def _silu(x):
    return x * jax.nn.sigmoid(x)


def _gelu(x):
    return 0.5 * x * (1.0 + lax.erf(x * (2.0 ** -0.5)))


def _rmsnorm_kernel(x_ref, g_ref, o_ref):
    o_ref[...] = _rms(x_ref[...], g_ref[...]).astype(o_ref.dtype)


def rmsnorm(x, g_stack, layer, out_dtype=BF16, tm=512):
    m, d = x.shape
    tm = _tile(m, tm)
    return pl.pallas_call(
        _rmsnorm_kernel,
        grid=(m // tm,),
        in_specs=[pl.BlockSpec((tm, d), lambda i: (i, 0)),
                  pl.BlockSpec((None, 1, d), lambda i: (layer, 0, 0))],
        out_specs=pl.BlockSpec((tm, d), lambda i: (i, 0)),
        out_shape=jax.ShapeDtypeStruct((m, d), out_dtype),
        compiler_params=_params("parallel"),
        name="rmsnorm",
    )(x, g_stack.reshape(g_stack.shape[0], 1, d))


_ROWS_PER_DOT = 1024


def _mm_ws_kernel(*refs, n_w, mode, w_rows_are_outputs, has_side_cast, has_prenorm):
    x_ref = refs[0]
    w_refs = refs[1:1 + n_w]
    rest = list(refs[1 + n_w:])
    gpre_ref = rest.pop(0) if has_prenorm else None
    if has_side_cast:
        side_ref, o_ref, side_out_ref, wb_ref = rest
    else:
        o_ref, wb_ref = rest
    tn = wb_ref.shape[1] // n_w

    @pl.when(pl.program_id(1) == 0)
    def _():
        for i, w_ref in enumerate(w_refs):
            w = w_ref[...]
            wb_ref[:, i * tn:(i + 1) * tn] = (w.T if w_rows_are_outputs else w).astype(BF16)
        if has_side_cast:
            side_out_ref[...] = side_ref[...].astype(BF16)

    tm = x_ref.shape[0]
    for r0 in range(0, tm, min(tm, _ROWS_PER_DOT)):
        rows = slice(r0, r0 + min(tm, _ROWS_PER_DOT))
        x = x_ref[rows, :]
        if has_prenorm:
            x = _rms(x, gpre_ref[...]).astype(BF16)
        y = jnp.dot(x, wb_ref[...], preferred_element_type=F32)
        if mode == "swiglu":
            o_ref[rows, :] = (_silu(y[:, :tn]) * y[:, tn:]).astype(o_ref.dtype)
        elif mode == "headmajor":
            for c in range(tn // LANES):
                o_ref[c, rows, :] = y[:, c * LANES:(c + 1) * LANES].astype(o_ref.dtype)
        else:
            o_ref[rows, :] = y.astype(o_ref.dtype)


def matmul_ws(x, w_stack, layer, *, mode="plain", out_dtype=BF16, tm=1024, tn=512, n_cols=None, col_offsets=(0,),
              w_rows_are_outputs=False, side_cast=None, prenorm=None):
    m, k = x.shape
    n = w_stack.shape[1 if w_rows_are_outputs else 2] if n_cols is None else n_cols
    tm, tn = _tile(m, tm), _tile(n, tn)
    n_w = len(col_offsets)
    assert all(off % tn == 0 for off in col_offsets)
    n_col_tiles = n // tn

    def w_spec(off):
        if w_rows_are_outputs:
            return pl.BlockSpec((None, tn, k), lambda j, i: (layer, j + off // tn, 0))
        return pl.BlockSpec((None, k, tn), lambda j, i: (layer, 0, j + off // tn))

    if mode == "headmajor":
        out_shape = jax.ShapeDtypeStruct((n // LANES, m, LANES), out_dtype)
        out_spec = pl.BlockSpec((tn // LANES, tm, LANES), lambda j, i: (j, i, 0))
    else:
        out_shape = jax.ShapeDtypeStruct((m, n), out_dtype)
        out_spec = pl.BlockSpec((tm, tn), lambda j, i: (i, j))
    in_specs = [pl.BlockSpec((tm, k), lambda j, i: (i, 0))] + [w_spec(off) for off in col_offsets]
    args = [x] + [w_stack] * n_w
    if prenorm is not None:
        g_stack, g_layer = prenorm
        in_specs.append(pl.BlockSpec((None, 1, k), lambda j, i: (g_layer, 0, 0)))
        args.append(g_stack.reshape(g_stack.shape[0], 1, k))
    if side_cast is not None:
        _, side_rows, side_cols = side_cast.shape
        chunk = side_rows // n_col_tiles
        assert chunk * n_col_tiles == side_rows
        in_specs.append(pl.BlockSpec((None, chunk, side_cols), lambda j, i: (layer, j, 0)))
        args.append(side_cast)
        out_spec = [out_spec, pl.BlockSpec((chunk, side_cols), lambda j, i: (j, 0))]
        out_shape = [out_shape, jax.ShapeDtypeStruct((side_rows, side_cols), BF16)]
    return pl.pallas_call(
        functools.partial(_mm_ws_kernel, n_w=n_w, mode=mode, w_rows_are_outputs=w_rows_are_outputs,
                          has_side_cast=side_cast is not None, has_prenorm=prenorm is not None),
        grid=(n_col_tiles, m // tm),
        in_specs=in_specs,
        out_specs=out_spec,
        out_shape=out_shape,
        scratch_shapes=[pltpu.VMEM((k, n_w * tn), BF16)],
        compiler_params=_params("arbitrary", "arbitrary"),
        name="mm_ws_" + mode,
    )(*args)


def _mm_post_kernel(a_ref, w_ref, x_ref, gp_ref, *rest, n_sub, scale, has_next):
    if has_next:
        gn_ref, x_out_ref, xn_out_ref = rest
    else:
        gn_ref = xn_out_ref = None
        (x_out_ref,) = rest
    ts = x_ref.shape[0] // n_sub
    for s in range(n_sub):
        rows = slice(s * ts, (s + 1) * ts)
        f = jnp.dot(a_ref[rows, :], w_ref[...], preferred_element_type=F32)
        x_new = x_ref[rows, :] + scale * _rms(f, gp_ref[...])
        x_out_ref[rows, :] = x_new
        if has_next:
            xn_out_ref[rows, :] = _rms(x_new, gn_ref[...]).astype(xn_out_ref.dtype)


def matmul_post(act, w_b, x, g_post, g_next, *, scale, tm, n_sub):
    m, d = x.shape
    k = act.shape[1]
    tm = _tile(m, tm)
    has_next = g_next is not None
    assert w_b.shape == (k, d) and w_b.dtype == BF16
    row = pl.BlockSpec((tm, d), lambda i: (i, 0))

    def vec(pair):
        stack, lyr = pair
        return stack.reshape(stack.shape[0], 1, d), pl.BlockSpec((None, 1, d), lambda i: (lyr, 0, 0))

    gp_arr, gp_spec = vec(g_post)
    in_specs = [pl.BlockSpec((tm, k), lambda i: (i, 0)),
                pl.BlockSpec((k, d), lambda i: (0, 0), pipeline_mode=pl.Buffered(1)),
                row, gp_spec]
    args = [act, w_b, x, gp_arr]
    out_shape = [jax.ShapeDtypeStruct((m, d), F32)]
    out_specs = [row]
    if has_next:
        gn_arr, gn_spec = vec(g_next)
        in_specs.append(gn_spec)
        args.append(gn_arr)
        out_shape.append(jax.ShapeDtypeStruct((m, d), BF16))
        out_specs.append(row)
    outs = pl.pallas_call(
        functools.partial(_mm_post_kernel, n_sub=n_sub, scale=scale, has_next=has_next),
        grid=(m // tm,),
        in_specs=in_specs,
        out_specs=out_specs,
        out_shape=out_shape,
        compiler_params=_params("parallel"),
        name="mm_post",
    )(*args)
    return (outs[0], outs[1]) if has_next else (outs[0], None)


def _sgu_rows(u_ref, v_ref, w_ref, bt_ref, lg_ref, lb_ref, r0, n_chunks):
    c = GM_CHUNK
    rows = slice(r0, r0 + n_chunks * c)
    width = GM_HEADS * HEAD_DIM
    row = lax.broadcasted_iota(jnp.int32, (c, c), 0)
    col = lax.broadcasted_iota(jnp.int32, (c, c), 1)
    causal = row >= col
    vg = [_gelu(v_ref[h, rows, :].astype(F32)) for h in range(GM_HEADS)]
    mu = jnp.sum(sum(vg), axis=-1, keepdims=True) * (1.0 / width)
    var = jnp.sum(sum(jnp.square(x - mu) for x in vg), axis=-1, keepdims=True) * (1.0 / width)
    rstd = lax.rsqrt(var + NORM_EPS)
    heads = []
    for h in range(GM_HEADS):
        vn = ((vg[h] - mu) * rstd * lg_ref[h] + lb_ref[h]).astype(BF16)
        w = jnp.where(causal, w_ref[h], 0.0).astype(BF16)
        rhs = jnp.concatenate([vn[n * c:(n + 1) * c] for n in range(n_chunks)], axis=1)
        mixed = jnp.dot(w, rhs, preferred_element_type=F32) + bt_ref[:, h:h + 1]
        mixed = jnp.concatenate([mixed[:, n * HEAD_DIM:(n + 1) * HEAD_DIM] for n in range(n_chunks)], axis=0)
        heads.append((_gelu(u_ref[h, rows, :].astype(F32)) * mixed).astype(BF16))
    return jnp.concatenate(heads, axis=-1)


def _mix_out_kernel(u_ref, v_ref, sw_ref, bt_ref, lg_ref, lb_ref, yb_ref, w_ref, x_ref, gp_ref, gn_ref,
                    x_out_ref, xn_out_ref, wb_ref, *, n_cast, n_sub):
    step = pl.program_id(0)
    ck = w_ref.shape[0]
    ka = GM_HEADS * HEAD_DIM

    @pl.when(step < n_cast)
    def _():
        wb_ref[pl.ds(pl.multiple_of(step * ck, ck), ck), :] = w_ref[...].astype(BF16)

    @pl.when(step >= n_cast)
    def _():
        ts = x_ref.shape[0] // n_sub
        for s in range(n_sub):
            rows = slice(s * ts, (s + 1) * ts)
            y_a = _sgu_rows(u_ref, v_ref, sw_ref, bt_ref, lg_ref, lb_ref, s * ts, ts // GM_CHUNK)
            f = (jnp.dot(y_a, wb_ref[0:ka, :], preferred_element_type=F32)
                 + jnp.dot(yb_ref[rows, :], wb_ref[ka:, :], preferred_element_type=F32))
            x_new = x_ref[rows, :] + _rms(f, gp_ref[...])
            x_out_ref[rows, :] = x_new
            xn_out_ref[rows, :] = _rms(x_new, gn_ref[...]).astype(xn_out_ref.dtype)


def mix_out(proj_hm, y_b, sm_w, sm_b, ln_g, ln_b, w_out, layer, x, g_post, g_next, *, tm=512, n_sub=4, ck=512):
    m, d = x.shape
    tm = _tile(m, tm)
    depth = sm_w.shape[0]
    k_all = w_out.shape[1]
    ck = _tile(k_all, ck)
    n_cast = k_all // ck

    def tile_idx(i):
        return jnp.maximum(i - n_cast, 0)

    row = pl.BlockSpec((tm, d), lambda i: (tile_idx(i), 0))

    def vec(pair):
        stack, lyr = pair
        return stack.reshape(stack.shape[0], 1, d), pl.BlockSpec((None, 1, d), lambda i: (lyr, 0, 0))

    gp_arr, gp_spec = vec(g_post)
    gn_arr, gn_spec = vec(g_next)
    outs = pl.pallas_call(
        functools.partial(_mix_out_kernel, n_cast=n_cast, n_sub=n_sub),
        grid=(n_cast + m // tm,),
        in_specs=[pl.BlockSpec((GM_HEADS, tm, HEAD_DIM), lambda i: (_U_GRP // GM_HEADS, tile_idx(i), 0)),
                  pl.BlockSpec((GM_HEADS, tm, HEAD_DIM), lambda i: (_V_GRP // GM_HEADS, tile_idx(i), 0)),
                  pl.BlockSpec((None, GM_HEADS, GM_CHUNK, GM_CHUNK), lambda i: (layer, 0, 0, 0)),
                  pl.BlockSpec((None, GM_CHUNK, GM_HEADS), lambda i: (layer, 0, 0)),
                  pl.BlockSpec((None, GM_HEADS, 1, HEAD_DIM), lambda i: (layer, 0, 0, 0)),
                  pl.BlockSpec((None, GM_HEADS, 1, HEAD_DIM), lambda i: (layer, 0, 0, 0)),
                  pl.BlockSpec((tm, y_b.shape[1]), lambda i: (tile_idx(i), 0)),
                  pl.BlockSpec((None, ck, d), lambda i: (layer, jnp.minimum(i, n_cast - 1), 0)),
                  row, gp_spec, gn_spec],
        out_specs=[row, row],
        out_shape=[jax.ShapeDtypeStruct((m, d), F32), jax.ShapeDtypeStruct((m, d), BF16)],
        scratch_shapes=[pltpu.VMEM((k_all, d), BF16)],
        compiler_params=_params("arbitrary"),
        name="mix_out",
    )(proj_hm, proj_hm, sm_w, sm_b.transpose(0, 2, 1),
      ln_g.reshape(depth, GM_HEADS, 1, HEAD_DIM), ln_b.reshape(depth, GM_HEADS, 1, HEAD_DIM),
      y_b, w_out, x, gp_arr, gn_arr)
    return outs[0], outs[1]


def _gates_kernel(x_ref, w_ref, alog_ref, dtb_ref, g_ref, gt_ref, wb_ref, *, n_chunks):
    c = DN_CHUNK

    @pl.when(pl.program_id(0) == 0)
    def _():
        w = w_ref[...].T
        lane_w = lax.broadcasted_iota(jnp.int32, w.shape, 1)
        w_valid = jnp.where(lane_w < 2 * DN_HEADS, w, 0.0)
        w_decay = jnp.where(lane_w >= DN_HEADS, w_valid, 0.0)
        w_all = w_valid
        for r in range(1, _N_DECAY_COPIES):
            w_all = w_all + pltpu.roll(w_decay, r * DN_HEADS, axis=1)
        wb_ref[...] = w_all.astype(BF16)

    ba = jnp.dot(x_ref[...], wb_ref[...], preferred_element_type=F32)
    grp = lax.broadcasted_iota(jnp.int32, ba.shape, 1) // DN_HEADS
    beta = jax.nn.sigmoid(ba)
    g = -jnp.exp(alog_ref[...]) * jax.nn.softplus(ba + dtb_ref[...])
    row = lax.broadcasted_iota(jnp.int32, (c, c), 0)
    col = lax.broadcasted_iota(jnp.int32, (c, c), 1)
    tri = (row >= col).astype(F32)
    gcum_chunks, last_chunks = [], []
    for n in range(n_chunks):
        gc = jnp.dot(tri, g[n * c:(n + 1) * c], preferred_element_type=F32, precision=lax.Precision.HIGHEST)
        gcum_chunks.append(gc)
        last_chunks.append(jnp.broadcast_to(gc[c - 1:c, :], gc.shape))
    gcum = jnp.concatenate(gcum_chunks, axis=0)
    glast = jnp.concatenate(last_chunks, axis=0)
    out = jnp.where(grp == _BETA, beta,
                    jnp.where(grp == _GCUM, gcum,
                              jnp.where(grp == _EG, jnp.exp(gcum),
                                        jnp.where(grp == _EDEC, jnp.exp(glast - gcum), 0.0))))
    g_ref[...] = out
    for n in range(n_chunks):
        twice_t = jnp.concatenate([gcum_chunks[n], gcum_chunks[n]], axis=0).T
        for h in range(DN_HEADS):
            lane_idx = _GCUM * DN_HEADS + h
            gt_ref[h, n] = twice_t[lane_idx:lane_idx + 1, :]


def gates(xn, w_in_t, a_log, dt_bias, layer, *, tb=256):
    m, k = xn.shape
    tb = _tile(m, tb)
    depth = a_log.shape[0]
    tail_block = (w_in_t.shape[1] - 2 * DN_HEADS) // LANES
    assert tail_block * LANES + 2 * DN_HEADS == w_in_t.shape[1]

    def lane_row(p):
        z = jnp.zeros((depth, 1, DN_HEADS), F32)
        r = p.reshape(depth, 1, DN_HEADS)
        tail = jnp.zeros((depth, 1, LANES - (1 + _N_DECAY_COPIES) * DN_HEADS), F32)
        return jnp.concatenate([z] + [r] * _N_DECAY_COPIES + [tail], axis=-1)

    return pl.pallas_call(
        functools.partial(_gates_kernel, n_chunks=tb // DN_CHUNK),
        grid=(m // tb,),
        in_specs=[pl.BlockSpec((tb, k), lambda i: (i, 0)),
                  pl.BlockSpec((None, LANES, k), lambda i: (layer, tail_block, 0)),
                  pl.BlockSpec((None, 1, LANES), lambda i: (layer, 0, 0)),
                  pl.BlockSpec((None, 1, LANES), lambda i: (layer, 0, 0))],
        out_specs=[pl.BlockSpec((tb, LANES), lambda i: (i, 0)),
                   pl.BlockSpec((DN_HEADS, tb // DN_CHUNK, 1, 2 * DN_CHUNK), lambda i: (0, i, 0, 0))],
        out_shape=[jax.ShapeDtypeStruct((m, LANES), F32),
                   jax.ShapeDtypeStruct((DN_HEADS, m // DN_CHUNK, 1, 2 * DN_CHUNK), F32)],
        scratch_shapes=[pltpu.VMEM((k, LANES), BF16)],
        compiler_params=_params("arbitrary"),
        name="dn_gates",
    )(xn, w_in_t, lane_row(a_log), lane_row(dt_bias))


def _dot_nt(a, b):
    return lax.dot_general(a, b, (((1,), (1,)), ((), ())), preferred_element_type=F32)


def _bdot(a, b):
    return lax.dot_general(a, b, (((2,), (1,)), ((0,), (0,))), preferred_element_type=F32)


def _bdot_nt(a, b):
    return lax.dot_general(a, b, (((2,), (2,)), ((0,), (0,))), preferred_element_type=F32)


def _bdot_tn(a, b):
    return lax.dot_general(a, b, (((1,), (1,)), ((0,), (0,))), preferred_element_type=F32)


def _split(x):
    hi = x.astype(BF16)
    lo = (x - hi.astype(F32)).astype(BF16)
    return hi, lo


def _as_lhs(parts):
    hi, lo = parts
    return jnp.concatenate([hi, lo], axis=-1)


def _as_rhs(parts):
    hi, lo = parts
    return jnp.concatenate([hi, lo, hi, lo], axis=-2)


def _unit_lower_inverse2(n2):
    c = n2.shape[-2]
    row = lax.broadcasted_iota(jnp.int32, n2.shape[-2:], 0)
    col = lax.broadcasted_iota(jnp.int32, n2.shape[-2:], 1)
    eye2 = (row == (col % c)).astype(F32)
    inv = eye2 + n2
    parts = _split(n2)
    span = 2
    while span < c:
        n2 = _bdot(_as_lhs(parts), _as_rhs(parts))
        parts = _split(n2)
        inv = inv + _bdot(_as_lhs(_split(inv)), _as_rhs(parts))
        span *= 2
    return inv


def _deltanet_kernel(q_ref, k_ref, v_ref, z_ref, g_ref, gt_ref, cw_ref, nw_ref,
                     o_ref, state_ref, xbuf_ref, *, n_sub, sub_chunks):
    c = DN_CHUNK
    ts = sub_chunks * c
    tb = n_sub * ts
    hd = HEAD_DIM
    nh = DN_HEADS
    n_pair = nh * sub_chunks

    @pl.when(pl.program_id(1) == 0)
    def _():
        state_ref[...] = jnp.zeros_like(state_ref)
        xbuf_ref[:, 0:8, :] = jnp.zeros((3 * nh, 8, hd), F32)

    xbuf_ref[0:nh, 8:8 + tb, :] = q_ref[...].astype(F32)
    xbuf_ref[nh:2 * nh, 8:8 + tb, :] = k_ref[...].astype(F32)
    xbuf_ref[2 * nh:3 * nh, 8:8 + tb, :] = v_ref[...].astype(F32)

    def pairs(a):
        return a.reshape(n_pair, c, a.shape[-1])

    def unpair(a):
        return a.reshape(nh, sub_chunks, c, a.shape[-1])

    row2 = lax.broadcasted_iota(jnp.int32, (c, 2 * c), 0)
    col2 = lax.broadcasted_iota(jnp.int32, (c, 2 * c), 1) % c
    causal2 = row2 >= col2
    strict2 = row2 > col2

    def chunk_local(s):
        r0 = s * ts
        y = cw_ref[:, 0:1, :] * xbuf_ref[:, 5 + r0:5 + r0 + ts, :]
        for i in range(1, CONV_WIDTH):
            y = y + cw_ref[:, i:i + 1, :] * xbuf_ref[:, 5 + i + r0:5 + i + r0 + ts, :]
        y = _silu(y)
        q, k, v = y[0:nh], y[nh:2 * nh], y[2 * nh:3 * nh]
        qn = q * lax.rsqrt(jnp.sum(q * q, axis=-1, keepdims=True) + NORM_EPS) * (hd ** -0.5)
        kn = k * lax.rsqrt(jnp.sum(k * k, axis=-1, keepdims=True) + NORM_EPS)

        gates_blk = g_ref[r0:r0 + ts, :]
        lane = lax.broadcasted_iota(jnp.int32, gates_blk.shape, 1)

        def gate_cols(grp):
            return jnp.stack([jnp.sum(jnp.where(lane == grp * nh + h, gates_blk, 0.0), axis=-1, keepdims=True)
                              for h in range(nh)], axis=0)

        beta, gc, eg, edec = (gate_cols(grp) for grp in (_BETA, _GCUM, _EG, _EDEC))
        gr2 = gt_ref[:, s * sub_chunks:(s + 1) * sub_chunks].reshape(n_pair, 1, 2 * c)
        decay2 = jnp.where(causal2, jnp.exp(jnp.where(causal2, pairs(gc) - gr2, 0.0)), 0.0)
        kb = kn * beta
        kn_b = pairs(kn).astype(BF16)
        kkt2 = _bdot_nt(pairs(kb).astype(BF16), jnp.concatenate([kn_b, kn_b], axis=1)) * decay2
        inv2 = _unit_lower_inverse2(jnp.where(strict2, -kkt2, 0.0))
        rhs = pairs(jnp.concatenate([v * beta, kb * eg], axis=-1))
        sol = unpair(_bdot(_as_lhs(_split(inv2)), _as_rhs(_split(rhs))))
        qk = _bdot_nt(pairs(qn).astype(BF16), kn_b) * decay2[..., :c]
        return dict(u=sol[..., :hd], w=sol[..., hd:], qk=unpair(qk), qe=unpair(pairs(qn * eg)),
                    kd=unpair(pairs(kn * edec)), eg=unpair(pairs(eg)))

    local = [chunk_local(s) for s in range(n_sub)]
    xbuf_ref[:, 0:8, :] = xbuf_ref[:, tb:tb + 8, :]

    state = state_ref[...]
    for s in range(n_sub):
        part = local[s]
        for n in range(sub_chunks):
            r0 = s * ts + n * c
            state_b = state.astype(BF16)
            ws_qs = _bdot(jnp.concatenate([part["w"][:, n].astype(BF16), part["qe"][:, n].astype(BF16)], axis=1),
                          state_b)
            v_new = part["u"][:, n] - ws_qs[:, :c]
            v_new_b = v_new.astype(BF16)
            o_n = ws_qs[:, c:] + _bdot(part["qk"][:, n].astype(BF16), v_new_b)
            state = state * part["eg"][:, n, c - 1:c, :] + _bdot_tn(part["kd"][:, n].astype(BF16), v_new_b)
            out = _rms(o_n, nw_ref[...]) * _silu(z_ref[:, r0:r0 + c, :].astype(F32))
            for h in range(nh):
                o_ref[r0:r0 + c, h * hd:(h + 1) * hd] = out[h].astype(o_ref.dtype)
    state_ref[...] = state


def deltanet(proj_hm, gates_arr, gates_t, conv_w, dn_norm_w, layer, *, batch, tb=512, ts=256):
    _, m, _ = proj_hm.shape
    t_len = m // batch
    tb = _tile(t_len, tb)
    ts = _tile(tb, ts)
    n_t = t_len // tb
    hd = HEAD_DIM
    depth = conv_w.shape[0]
    cw = conv_w.reshape(depth, CONV_WIDTH, 3 * DN_HEADS, hd).transpose(0, 2, 1, 3)

    def grp(base):
        return pl.BlockSpec((DN_HEADS, tb, hd), lambda b, t: (base // DN_HEADS, b * n_t + t, 0))

    return pl.pallas_call(
        functools.partial(_deltanet_kernel, n_sub=tb // ts, sub_chunks=ts // DN_CHUNK),
        grid=(batch, n_t),
        in_specs=[grp(_Q_GRP), grp(_K_GRP), grp(_VV_GRP), grp(_Z_GRP),
                  pl.BlockSpec((tb, LANES), lambda b, t: (b * n_t + t, 0)),
                  pl.BlockSpec((DN_HEADS, tb // DN_CHUNK, 1, 2 * DN_CHUNK), lambda b, t: (0, b * n_t + t, 0, 0)),
                  pl.BlockSpec((None, 3 * DN_HEADS, CONV_WIDTH, hd), lambda b, t: (layer, 0, 0, 0)),
                  pl.BlockSpec((None, 1, hd), lambda b, t: (layer, 0, 0))],
        out_specs=pl.BlockSpec((tb, DN_HEADS * hd), lambda b, t: (b * n_t + t, 0)),
        out_shape=jax.ShapeDtypeStruct((m, DN_HEADS * hd), BF16),
        scratch_shapes=[pltpu.VMEM((DN_HEADS, hd, hd), F32),
                        pltpu.VMEM((3 * DN_HEADS, tb + 8, hd), F32)],
        compiler_params=_params("parallel", "arbitrary"),
        name="deltanet",
    )(proj_hm, proj_hm, proj_hm, proj_hm, gates_arr, gates_t, cw, dn_norm_w.reshape(depth, 1, hd))


def _xattn_fold_kernel(kv_ref, w_ref, o_ref, wb_ref, *, n_cast, fold_keys, head_dim, n_mem):
    step = pl.program_id(0)
    ck = w_ref.shape[0]

    @pl.when(step < n_cast)
    def _():
        wb_ref[pl.ds(pl.multiple_of(step * ck, ck), ck), :] = w_ref[...].astype(BF16)

    @pl.when(step >= n_cast)
    def _():
        for h in range(XA_HEADS):
            feat = slice(h * head_dim, (h + 1) * head_dim)
            toks = slice(h * n_mem, (h + 1) * n_mem)
            if fold_keys:
                o_ref[:, toks] = _dot_nt(wb_ref[:, feat], kv_ref[:, feat]).astype(o_ref.dtype)
            else:
                o_ref[toks, :] = jnp.dot(kv_ref[:, feat], wb_ref[feat, :],
                                         preferred_element_type=F32).astype(o_ref.dtype)


def xattn_fold(kv, w_stack, layer, *, batch, fold_keys, ck=512):
    n_mem = kv.shape[0] // batch
    d = w_stack.shape[1]
    ck = _tile(d, ck)
    n_cast = d // ck
    out_dims = (d, XA_HEADS * n_mem) if fold_keys else (XA_HEADS * n_mem, d)
    return pl.pallas_call(
        functools.partial(_xattn_fold_kernel, n_cast=n_cast, fold_keys=fold_keys, head_dim=d // XA_HEADS,
                          n_mem=n_mem),
        grid=(n_cast + batch,),
        in_specs=[pl.BlockSpec((n_mem, d), lambda i: (jnp.maximum(i - n_cast, 0), 0 if fold_keys else 1)),
                  pl.BlockSpec((None, ck, d), lambda i: (layer, jnp.minimum(i, n_cast - 1), 0))],
        out_specs=pl.BlockSpec((None,) + out_dims, lambda i: (jnp.maximum(i - n_cast, 0), 0, 0)),
        out_shape=jax.ShapeDtypeStruct((batch,) + out_dims, BF16),
        scratch_shapes=[pltpu.VMEM((d, d), BF16)],
        compiler_params=_params("arbitrary"),
        name="xattn_fold",
    )(kv, w_stack)


def _xattn_post_kernel(xn_ref, wk_ref, vw_ref, x_ref, gp_ref, gn_ref, x_out_ref, xn_out_ref, *,
                       n_sub, head_dim, n_mem):
    ts = x_ref.shape[0] // n_sub
    scale = head_dim ** -0.5
    for s in range(n_sub):
        rows = slice(s * ts, (s + 1) * ts)
        logits = jnp.dot(xn_ref[rows, :], wk_ref[...], preferred_element_type=F32) * scale
        probs = []
        for h in range(XA_HEADS):
            l_h = logits[:, h * n_mem:(h + 1) * n_mem]
            e = jnp.exp(l_h - jnp.max(l_h, axis=-1, keepdims=True))
            probs.append((e / jnp.sum(e, axis=-1, keepdims=True)).astype(BF16))
        c = jnp.dot(jnp.concatenate(probs, axis=-1), vw_ref[...], preferred_element_type=F32)
        x_new = x_ref[rows, :] + _rms(c, gp_ref[...])
        x_out_ref[rows, :] = x_new
        xn_out_ref[rows, :] = _rms(x_new, gn_ref[...]).astype(xn_out_ref.dtype)


def xattn_post(xn, wk, vw, x, g_post, g_next, *, batch, tm=512, n_sub=2):
    m, d = x.shape
    t_len = m // batch
    tm = _tile(t_len, tm)
    tiles_per_seq = t_len // tm
    n_mem = wk.shape[2] // XA_HEADS
    row = pl.BlockSpec((tm, d), lambda i: (i, 0))

    def vec(pair):
        stack, lyr = pair
        return stack.reshape(stack.shape[0], 1, d), pl.BlockSpec((None, 1, d), lambda i: (lyr, 0, 0))

    gp_arr, gp_spec = vec(g_post)
    gn_arr, gn_spec = vec(g_next)
    return pl.pallas_call(
        functools.partial(_xattn_post_kernel, n_sub=n_sub, head_dim=d // XA_HEADS, n_mem=n_mem),
        grid=(m // tm,),
        in_specs=[row,
                  pl.BlockSpec((None,) + wk.shape[1:], lambda i: (i // tiles_per_seq, 0, 0)),
                  pl.BlockSpec((None,) + vw.shape[1:], lambda i: (i // tiles_per_seq, 0, 0)),
                  row, gp_spec, gn_spec],
        out_specs=[row, row],
        out_shape=[jax.ShapeDtypeStruct((m, d), F32), jax.ShapeDtypeStruct((m, d), BF16)],
        compiler_params=_params("parallel"),
        name="xattn_post",
    )(xn, wk, vw, x, gp_arr, gn_arr)


def kernel(x, mem, ffn1_norm_pre, ffn1_w_gate_up, ffn1_w_down, ffn1_norm_post, mix_norm_pre, w_in, conv_w, a_log, dt_bias, sm_w, sm_b, sm_ln_g, sm_ln_b, dn_norm_w, w_out, mix_norm_post, xa_norm_pre, mem_norm, w_xq, w_xkv, w_xo, xa_norm_post, ffn2_norm_pre, ffn2_w_gate_up, ffn2_w_down, ffn2_norm_post):
    batch, t_len, d = x.shape
    depth = w_in.shape[0]
    d_ff = ffn1_w_down.shape[1]
    n_main = _N_GRP * LANES
    x2 = x.reshape(batch * t_len, d)
    mem2 = mem.reshape(-1, d)
    w_in_t = jnp.swapaxes(w_in, 1, 2)

    def ffn(xn, x2, w_gate_up, w_down, l, g_post, g_next):
        hmid, w_down_b = matmul_ws(xn, w_gate_up, l, mode="swiglu", n_cols=d_ff, col_offsets=(0, d_ff), tm=2048,
                                   side_cast=w_down)
        return matmul_post(hmid, w_down_b, x2, g_post, g_next, scale=0.5, tm=512, n_sub=4)

    xn = rmsnorm(x2, ffn1_norm_pre, 0)
    for l in range(depth):
        x2, xn = ffn(xn, x2, ffn1_w_gate_up, ffn1_w_down, l, (ffn1_norm_post, l), (mix_norm_pre, l))

        proj_hm = matmul_ws(xn, w_in_t, l, mode="headmajor", n_cols=n_main, tm=2048, tn=1024,
                            w_rows_are_outputs=True)
        gates_arr, gates_t = gates(xn, w_in_t, a_log, dt_bias, l)
        y_b = deltanet(proj_hm, gates_arr, gates_t, conv_w, dn_norm_w, l, batch=batch)
        x2, xn = mix_out(proj_hm, y_b, sm_w, sm_b, sm_ln_g, sm_ln_b, w_out, l, x2,
                         (mix_norm_post, l), (xa_norm_pre, l))

        kv = matmul_ws(mem2, w_xkv, l, prenorm=(mem_norm, l))
        wk = xattn_fold(kv, w_xq, l, batch=batch, fold_keys=True)
        vw = xattn_fold(kv, w_xo, l, batch=batch, fold_keys=False)
        x2, xn = xattn_post(xn, wk, vw, x2, (xa_norm_post, l), (ffn2_norm_pre, l), batch=batch)

        g_next = (ffn1_norm_pre, l + 1) if l + 1 < depth else None
        x2, xn = ffn(xn, x2, ffn2_w_gate_up, ffn2_w_down, l, (ffn2_norm_post, l), g_next)
    return x2.reshape(batch, t_len, d)
```
